```python
import jax, jax.numpy as jnp
from jax import lax
import numpy as np

D_MODEL = 1024
BATCH = 16
SEQ = 2048
DEPTH = 4
DEC_BATCH = 128
DEC_SEQ = 8
PAST_LEN = 8192
PAGE_SIZE = 128

N_META = 16
EPS = 1e-6
POOL_WINDOWS = (2, 4, 8, 16)
N_POOL_GROUPS = len(POOL_WINDOWS)
POOL_GROUP = D_MODEL // N_POOL_GROUPS
POOL_BUF = max(POOL_WINDOWS) - 1
N_HEADS = 16
Q_LORA = 384
KV_LORA = 256
QK_NOPE = 64
QK_ROPE = 32
V_HEAD = 64
ROPE_BASE = 10000.0
Q_BLOCK = 128
ATTN_SCALE = (QK_NOPE + QK_ROPE) ** -0.5
D_FF = 2816
N_EXPERTS = 8
TOP_K = 2
N_POOL_LAYERS = (DEPTH + 1) // 2
N_ATTN_LAYERS = DEPTH // 2

kernel_name = 'hybrid_pool_mla_moe_step'


def rmsnorm(x, g):
    xf = x.astype(jnp.float32)
    y = xf * lax.rsqrt(jnp.mean(xf * xf, axis=-1, keepdims=True) + EPS)
    return (y * g.astype(jnp.float32)).astype(x.dtype)


def rope(x, pos):
    half = QK_ROPE // 2
    freqs = 1.0 / (ROPE_BASE ** (jnp.arange(half, dtype=jnp.float32) * (2.0 / QK_ROPE)))
    ang = pos.astype(jnp.float32)[:, None] * freqs[None, :]
    shape = (ang.shape[0],) + (1,) * (x.ndim - 3) + (half,)
    cos = jnp.cos(ang).reshape(shape)
    sin = jnp.sin(ang).reshape(shape)
    xf = x.astype(jnp.float32)
    x1, x2 = xf[..., :half], xf[..., half:]
    return jnp.concatenate([x1 * cos - x2 * sin, x2 * cos + x1 * sin], axis=-1).astype(x.dtype)


def multiscale_pool(h_ext, n_out, w_pool, scale):
    B, L, D = h_ext.shape
    cs = jnp.cumsum(h_ext.astype(jnp.float32), axis=1)
    cs = jnp.concatenate([jnp.zeros((B, 1, D), jnp.float32), cs], axis=1)
    hi = jnp.arange(L - n_out + 1, L + 1)
    h_out = h_ext[:, L - n_out:].astype(jnp.float32)
    groups = []
    for g, w in enumerate(POOL_WINDOWS):
        c0, c1 = g * POOL_GROUP, (g + 1) * POOL_GROUP
        csg = cs[:, :, c0:c1]
        lo = jnp.maximum(hi - w, 0)
        cnt = (hi - lo).astype(jnp.float32)[None, :, None]
        mean = (csg[:, hi] - csg[:, lo]) / cnt
        groups.append(mean - h_out[:, :, c0:c1])
    d = jnp.stack(groups, axis=2).astype(h_ext.dtype)
    y = jnp.einsum('btgc,gcd->btgd', d, w_pool).reshape(B, n_out, D)
    return y * scale


def mla_project(h, pos, w_in, g_q, g_kv, w_uq):
    B, T, _ = h.shape
    proj = h @ w_in
    cq, ckv, kpe = jnp.split(proj, [Q_LORA, Q_LORA + KV_LORA], axis=-1)
    q = (rmsnorm(cq, g_q) @ w_uq).reshape(B, T, N_HEADS, QK_NOPE + QK_ROPE)
    q_nope = q[..., :QK_NOPE]
    q_pe = rope(q[..., QK_NOPE:], pos)
    return q_nope, q_pe, rmsnorm(ckv, g_kv), rope(kpe, pos)


def mla_attend(q_nope, q_pe, q_pos, k_pos, ckv, kpe, w_uk, w_uv):
    q_lat = jnp.einsum('bqhn,chn->bqhc', q_nope, w_uk)
    s = jnp.einsum('bqhc,btc->bhqt', q_lat, ckv) + jnp.einsum('bqhr,btr->bhqt', q_pe, kpe)
    s = s.astype(jnp.float32) * ATTN_SCALE
    mask = k_pos[None, :] <= q_pos[:, None]
    s = jnp.where(mask[None, None], s, -1e30)
    p = jax.nn.softmax(s, axis=-1).astype(ckv.dtype)
    o_lat = jnp.einsum('bhqt,btc->bqhc', p, ckv)
    return jnp.einsum('bqhc,chv->bqhv', o_lat, w_uv)


def mla_attention(q_nope, q_pe, q_pos, ckv, kpe, w_uk, w_uv, w_o):
    B, Tq = q_nope.shape[:2]
    blk = min(Q_BLOCK, Tq)
    nblk = -(-Tq // blk)
    pad = nblk * blk - Tq
    padq = lambda a: jnp.pad(a, ((0, 0), (0, pad)) + ((0, 0),) * (a.ndim - 2))
    qn = padq(q_nope).reshape(B, nblk, blk, N_HEADS, QK_NOPE).swapaxes(0, 1)
    qp = padq(q_pe).reshape(B, nblk, blk, N_HEADS, QK_ROPE).swapaxes(0, 1)
    qpos = jnp.pad(q_pos, (0, pad), mode='edge').reshape(nblk, blk)
    k_pos = jnp.arange(ckv.shape[1], dtype=jnp.int32)

    def one_block(args):
        qn_b, qp_b, qpos_b = args
        return mla_attend(qn_b, qp_b, qpos_b, k_pos, ckv, kpe, w_uk, w_uv)

    o = lax.map(one_block, (qn, qp, qpos))
    o = o.swapaxes(0, 1).reshape(B, nblk * blk, N_HEADS * V_HEAD)[:, :Tq]
    return o @ w_o


def swiglu(h, w_gate, w_up, w_down):
    return (jax.nn.silu(h @ w_gate) * (h @ w_up)) @ w_down


def moe_swiglu(h, w_router, b_router, w_gate, w_up, w_down):
    shp = h.shape
    t = h.reshape(-1, shp[-1])
    logits = (t @ w_router).astype(jnp.float32)
    _, top_idx = lax.top_k(logits + b_router.astype(jnp.float32), TOP_K)
    top_w = jax.nn.softmax(jnp.take_along_axis(logits, top_idx, axis=-1), axis=-1)
    gates = jnp.sum(jax.nn.one_hot(top_idx, N_EXPERTS, dtype=jnp.float32) * top_w[..., None], axis=1).astype(h.dtype)
    out = jnp.zeros_like(t)
    for e in range(N_EXPERTS):
        out = out + gates[:, e:e + 1] * swiglu(t, w_gate[e], w_up[e], w_down[e])
    return out.reshape(shp)


def setup_inputs(seed: int = 0) -> dict:
    key = jax.random.key(seed)
    ks = jax.random.split(key, 27)
    f32 = jnp.float32
    n_pages = PAST_LEN // PAGE_SIZE
    n_used = DEC_BATCH * n_pages
    n_phys = n_used + n_used // 4 + 1

    def w(k, shape, fan_in):
        return jax.random.normal(k, shape, f32) * (fan_in ** -0.5)

    def gain(k, shape):
        return 1.0 + 0.02 * jax.random.normal(k, shape, f32)

    return {
        'x_prompt': jax.random.normal(ks[0], (BATCH, SEQ, D_MODEL), f32),
        'x_sample': jax.random.normal(ks[1], (DEC_BATCH, DEC_SEQ, D_MODEL), f32),
        'cache_ckv': jax.random.normal(ks[2], (N_ATTN_LAYERS, n_phys, PAGE_SIZE, KV_LORA), f32),
        'cache_kpe': jax.random.normal(ks[3], (N_ATTN_LAYERS, n_phys, PAGE_SIZE, QK_ROPE), f32),
        'state_pool': jax.random.normal(ks[4], (N_POOL_LAYERS, DEC_BATCH, POOL_BUF, D_MODEL), f32),
        'page_table': jax.random.permutation(ks[5], n_phys)[:n_used].reshape(DEC_BATCH, n_pages).astype(jnp.int32),
        'meta_tokens': jax.random.normal(ks[6], (N_META, D_MODEL), f32),
        'g_mix': gain(ks[7], (DEPTH, D_MODEL)),
        'g_ffn': gain(ks[8], (DEPTH, D_MODEL)),
        'g_final': gain(ks[9], (D_MODEL,)),
        'pool_w': w(ks[10], (N_POOL_LAYERS, N_POOL_GROUPS, POOL_GROUP, POOL_GROUP), POOL_GROUP),
        'pool_scale': gain(ks[11], (N_POOL_LAYERS, D_MODEL)),
        'mla_w_in': w(ks[12], (N_ATTN_LAYERS, D_MODEL, Q_LORA + KV_LORA + QK_ROPE), D_MODEL),
        'mla_g_q': gain(ks[13], (N_ATTN_LAYERS, Q_LORA)),
        'mla_g_kv': gain(ks[14], (N_ATTN_LAYERS, KV_LORA)),
        'mla_w_uq': w(ks[15], (N_ATTN_LAYERS, Q_LORA, N_HEADS * (QK_NOPE + QK_ROPE)), Q_LORA),
        'mla_w_uk': w(ks[16], (N_ATTN_LAYERS, KV_LORA, N_HEADS, QK_NOPE), KV_LORA),
        'mla_w_uv': w(ks[17], (N_ATTN_LAYERS, KV_LORA, N_HEADS, V_HEAD), KV_LORA),
        'mla_w_o': w(ks[18], (N_ATTN_LAYERS, N_HEADS * V_HEAD, D_MODEL), N_HEADS * V_HEAD),
        'ffn_w_gate': w(ks[19], (N_POOL_LAYERS, D_MODEL, D_FF), D_MODEL),
        'ffn_w_up': w(ks[20], (N_POOL_LAYERS, D_MODEL, D_FF), D_MODEL),
        'ffn_w_down': w(ks[21], (N_POOL_LAYERS, D_FF, D_MODEL), D_FF),
        'moe_w_router': w(ks[22], (N_ATTN_LAYERS, D_MODEL, N_EXPERTS), D_MODEL),
        'moe_b_router': 0.01 * jax.random.normal(ks[23], (N_ATTN_LAYERS, N_EXPERTS), f32),
        'moe_w_gate': w(ks[24], (N_ATTN_LAYERS, N_EXPERTS, D_MODEL, D_FF), D_MODEL),
        'moe_w_up': w(ks[25], (N_ATTN_LAYERS, N_EXPERTS, D_MODEL, D_FF), D_MODEL),
        'moe_w_down': w(ks[26], (N_ATTN_LAYERS, N_EXPERTS, D_FF, D_MODEL), D_FF),
    }


def reference(x_prompt, x_sample, cache_ckv, cache_kpe, state_pool, page_table, meta_tokens,
              g_mix, g_ffn, g_final, pool_w, pool_scale,
              mla_w_in, mla_g_q, mla_g_kv, mla_w_uq, mla_w_uk, mla_w_uv, mla_w_o,
              ffn_w_gate, ffn_w_up, ffn_w_down,
              moe_w_router, moe_b_router, moe_w_gate, moe_w_up, moe_w_down):
    B = x_prompt.shape[0]
    DB, S, D = x_sample.shape
    meta = jnp.broadcast_to(meta_tokens.astype(x_prompt.dtype)[None], (B, N_META, D))
    xp = jnp.concatenate([meta, x_prompt], axis=1)
    T = xp.shape[1]
    xs = x_sample
    pos_p = jnp.arange(T, dtype=jnp.int32)
    pos_s = PAST_LEN + jnp.arange(S, dtype=jnp.int32)

    ckv_p_rows, kpe_p_rows, pool_p_rows = [], [], []
    ckv_s_rows, kpe_s_rows, pool_s_rows = [], [], []
    for i in range(DEPTH):
        j = i // 2
        hp = rmsnorm(xp, g_mix[i])
        hs = rmsnorm(xs, g_mix[i])
        if i % 2 == 0:
            hs_ext = jnp.concatenate([state_pool[j].astype(hs.dtype), hs], axis=1)
            mp = multiscale_pool(hp, T, pool_w[j], pool_scale[j])
            ms = multiscale_pool(hs_ext, S, pool_w[j], pool_scale[j])
            pool_p_rows.append(hp[:, T - POOL_BUF:])
            pool_s_rows.append(hs_ext[:, hs_ext.shape[1] - POOL_BUF:])
        else:
            qn_p, qp_p, ckv_p, kpe_p = mla_project(hp, pos_p, mla_w_in[j], mla_g_q[j], mla_g_kv[j], mla_w_uq[j])
            mp = mla_attention(qn_p, qp_p, pos_p, ckv_p, kpe_p, mla_w_uk[j], mla_w_uv[j], mla_w_o[j])
            qn_s, qp_s, ckv_s, kpe_s = mla_project(hs, pos_s, mla_w_in[j], mla_g_q[j], mla_g_kv[j], mla_w_uq[j])
            past_ckv = cache_ckv[j][page_table].reshape(DB, -1, KV_LORA).astype(ckv_s.dtype)
            past_kpe = cache_kpe[j][page_table].reshape(DB, -1, QK_ROPE).astype(kpe_s.dtype)
            ckv_all = jnp.concatenate([past_ckv, ckv_s], axis=1)
            kpe_all = jnp.concatenate([past_kpe, kpe_s], axis=1)
            ms = mla_attention(qn_s, qp_s, pos_s, ckv_all, kpe_all, mla_w_uk[j], mla_w_uv[j], mla_w_o[j])
            ckv_p_rows.append(ckv_p)
            kpe_p_rows.append(kpe_p)
            ckv_s_rows.append(ckv_s)
            kpe_s_rows.append(kpe_s)
        xp = xp + mp
        xs = xs + ms
        hp = rmsnorm(xp, g_ffn[i])
        hs = rmsnorm(xs, g_ffn[i])
        if i % 2 == 0:
            xp = xp + swiglu(hp, ffn_w_gate[j], ffn_w_up[j], ffn_w_down[j])
            xs = xs + swiglu(hs, ffn_w_gate[j], ffn_w_up[j], ffn_w_down[j])
        else:
            xp = xp + moe_swiglu(hp, moe_w_router[j], moe_b_router[j], moe_w_gate[j], moe_w_up[j], moe_w_down[j])
            xs = xs + moe_swiglu(hs, moe_w_router[j], moe_b_router[j], moe_w_gate[j], moe_w_up[j], moe_w_down[j])

    y_prompt = rmsnorm(xp, g_final)[:, N_META:]
    y_sample = rmsnorm(xs, g_final)
    ckv_prompt = jnp.stack(ckv_p_rows)
    kpe_prompt = jnp.stack(kpe_p_rows)
    pool_prompt = jnp.stack(pool_p_rows)
    ckv_sample = jnp.stack(ckv_s_rows)
    kpe_sample = jnp.stack(kpe_s_rows)
    pool_sample = jnp.stack(pool_s_rows)
    return (y_prompt, y_sample, ckv_prompt, kpe_prompt, pool_prompt, ckv_sample, kpe_sample, pool_sample)
```

```python
import functools

import jax
import jax.numpy as jnp
from jax import lax
from jax.experimental import pallas as pl
from jax.experimental.pallas import tpu as pltpu

EPS = 1e-6
N_META = 16
POOL_WINDOWS = (2, 4, 8, 16)
N_HEADS = 16
Q_LORA = 384
KV_LORA = 256
QK_NOPE = 64
QK_ROPE = 32
V_HEAD = 64
ROPE_BASE = 10000.0
ATTN_SCALE = (QK_NOPE + QK_ROPE) ** -0.5
N_EXPERTS = 8
TOP_K = 2

HIST = 16
LANES = 128
NEG = -1e30
VMEM_LIMIT = 56 * 1024 * 1024
BF16 = jnp.bfloat16
F32 = jnp.float32


def _dot(a, b):
    return jnp.dot(a, b, preferred_element_type=F32)


def _dot_nt(a, b):
    return lax.dot_general(a, b, (((1,), (1,)), ((), ())), preferred_element_type=F32)


def _rms(x, g):
    return x * lax.rsqrt(jnp.mean(x * x, axis=-1, keepdims=True) + EPS) * g


def _params(sem, vmem=VMEM_LIMIT):
    return pltpu.CompilerParams(dimension_semantics=sem, vmem_limit_bytes=vmem)


def _const_spec(shape):
    nd = len(shape)
    return pl.BlockSpec(shape, lambda *_: (0,) * nd)


def _pool_kernel(*refs, has_hist, multi_tile, tb, tl):
    if multi_tile:
        x_ref, halo_ref, prev_ref, g_ref, w_ref, sc_ref, o_ref, tail_ref, hext_ref = refs
    else:
        x_ref, prev_ref, g_ref, w_ref, sc_ref, o_ref, tail_ref, hext_ref = refs
        halo_ref = None
    t = pl.program_id(1)
    d = x_ref.shape[-1]
    gc = d // len(POOL_WINDOWS)
    g = g_ref[...]
    x = x_ref[...]
    h = _rms(x, g)
    hext_ref[:, HIST:, :] = h
    prev = jnp.broadcast_to(prev_ref[...], (tb, HIST, d))
    if multi_tile:
        @pl.when(t == 0)
        def _():
            hext_ref[:, :HIST, :] = prev

        @pl.when(t > 0)
        def _():
            hext_ref[:, :HIST, :] = _rms(halo_ref[...], g)
    else:
        hext_ref[:, :HIST, :] = prev

    for gi, w in enumerate(POOL_WINDOWS):
        c0, c1 = gi * gc, (gi + 1) * gc
        win = hext_ref[:, pl.ds(HIST, tl), c0:c1]
        for i in range(1, w):
            win = win + hext_ref[:, pl.ds(HIST - i, tl), c0:c1]
        hg = hext_ref[:, pl.ds(HIST, tl), c0:c1]
        if has_hist:
            dg = win * (1.0 / w) - hg
        else:
            row = lax.broadcasted_iota(jnp.int32, (tb, tl, gc), 1) + t * tl
            cnt = jnp.minimum(row + 1, w).astype(F32)
            dg = win / cnt - hg
        dg = dg.reshape(tb * tl, gc).astype(BF16)
        yg = _dot(dg, w_ref[gi]).reshape(tb, tl, gc)
        o_ref[:, :, c0:c1] = x_ref[:, :, c0:c1] + yg * sc_ref[:, c0:c1]
    tail_ref[...] = hext_ref[:, pl.ds(tl, HIST), :]


def _pool_layer(x, prev, g, w_bf, scale, *, has_hist, tb, tl):
    b, l, d = x.shape
    n_t = l // tl
    multi = n_t > 1
    assert l % tl == 0 and b % tb == 0 and (not multi or (tb == 1 and tl % HIST == 0))
    shared = prev.shape[0] == 1
    per_tile = tl // HIST
    in_specs = [pl.BlockSpec((tb, tl, d), lambda i, t: (i, t, 0))]
    args = [x]
    if multi:
        in_specs.append(pl.BlockSpec((tb, HIST, d), lambda i, t: (i, jnp.maximum(t * per_tile - 1, 0), 0)))
        args.append(x)
    in_specs += [
        pl.BlockSpec((1 if shared else tb, HIST, d), (lambda i, t: (0, 0, 0)) if shared else (lambda i, t: (i, 0, 0))),
        _const_spec((1, d)), _const_spec(w_bf.shape), _const_spec((1, d)),
    ]
    args += [prev, g.reshape(1, d), w_bf, scale.reshape(1, d)]
    return pl.pallas_call(
        functools.partial(_pool_kernel, has_hist=has_hist, multi_tile=multi, tb=tb, tl=tl),
        grid=(b // tb, n_t),
        in_specs=in_specs,
        out_specs=[pl.BlockSpec((tb, tl, d), lambda i, t: (i, t, 0)),
                   pl.BlockSpec((tb, HIST, d), lambda i, t: (i, 0, 0))],
        out_shape=[jax.ShapeDtypeStruct((b, l, d), F32), jax.ShapeDtypeStruct((b, HIST, d), F32)],
        scratch_shapes=[pltpu.VMEM((tb, tl + HIST, d), F32)],
        compiler_params=_params(("parallel", "arbitrary")),
        name="pool_mixer",
    )(*args)


FF_CHUNK = 256


def _swiglu_into(acc_ref, h, wg_ref, wu_ref, wd_ref, lead=()):
    n_ff = wg_ref.shape[-1]
    for c in range(0, n_ff, FF_CHUNK):
        gg = _dot(h, wg_ref[lead + (slice(None), slice(c, c + FF_CHUNK))])
        uu = _dot(h, wu_ref[lead + (slice(None), slice(c, c + FF_CHUNK))])
        a = (gg * jax.nn.sigmoid(gg) * uu).astype(BF16)
        y = _dot(a, wd_ref[lead + (slice(c, c + FF_CHUNK), slice(None))])
        if c == 0:
            acc_ref[...] = y
        else:
            acc_ref[...] += y


def _ffn_kernel(x_ref, g_ref, wg_ref, wu_ref, wd_ref, o_ref, acc_ref):
    x = x_ref[...]
    h = _rms(x, g_ref[...]).astype(BF16)
    _swiglu_into(acc_ref, h, wg_ref, wu_ref, wd_ref)
    o_ref[...] = x + acc_ref[...]


def _ffn_dense(x, g, wg, wu, wd, *, tm):
    n, d = x.shape
    assert n % tm == 0
    resident = lambda shape: pl.BlockSpec(shape, lambda i: (0, 0), pipeline_mode=pl.Buffered(1))
    return pl.pallas_call(
        _ffn_kernel,
        grid=(n // tm,),
        in_specs=[pl.BlockSpec((tm, d), lambda i: (i, 0)), _const_spec((1, d)),
                  resident(wg.shape), resident(wu.shape), resident(wd.shape)],
        out_specs=pl.BlockSpec((tm, d), lambda i: (i, 0)),
        out_shape=jax.ShapeDtypeStruct((n, d), F32),
        scratch_shapes=[pltpu.VMEM((tm, d), F32)],
        compiler_params=_params(("parallel",)),
        name="ffn_dense",
    )(x, g.reshape(1, d), wg, wu, wd)


def _mla_proj_kernel(x_ref, cos_ref, sin_ref, g_ref, win_ref, gq_ref, gkv_ref, wqn_ref, wqa_ref, wqb_ref, wuk_ref,
                     qlat_ref, qpe_ref, ckv_ref, kpe_ref, ckvb_ref, kpeb_ref):
    h = _rms(x_ref[...], g_ref[...]).astype(BF16)
    proj = _dot(h, win_ref[...])
    c1, c2, c3 = Q_LORA, Q_LORA + KV_LORA, Q_LORA + KV_LORA + QK_ROPE
    cqn = _rms(proj[:, :c1], gq_ref[...]).astype(BF16)
    ckv = _rms(proj[:, c1:c2], gkv_ref[...])
    cos = cos_ref[...]
    sin = sin_ref[...]
    kpe = proj[:, c2:c3] * cos[:, :QK_ROPE] + proj[:, c3:c3 + QK_ROPE] * sin[:, :QK_ROPE]
    ckv_ref[...] = ckv
    kpe_ref[...] = kpe
    ckvb_ref[...] = ckv.astype(BF16)
    kpeb_ref[...] = kpe.astype(BF16)

    qn = _dot(cqn, wqn_ref[...]).astype(BF16)
    quad = 4 * QK_NOPE
    for p in range(N_HEADS // 4):
        ql = _dot(qn[:, p * quad:(p + 1) * quad], wuk_ref[p]) * ATTN_SCALE
        for j in range(4):
            qlat_ref[4 * p + j] = ql[:, j * KV_LORA:(j + 1) * KV_LORA].astype(qlat_ref.dtype)
    qr = (_dot(cqn, wqa_ref[...]) * cos + _dot(cqn, wqb_ref[...]) * sin) * ATTN_SCALE
    for hd in range(N_HEADS):
        qpe_ref[hd] = qr[:, hd * QK_ROPE:(hd + 1) * QK_ROPE].astype(qpe_ref.dtype)


def _mla_proj(x, cos, sin, g, w, *, tm, q_dtype):
    n, d = x.shape
    assert n % tm == 0 and cos.shape[0] % tm == 0
    n_tab = cos.shape[0] // tm
    hr = N_HEADS * QK_ROPE
    tok = lambda width: pl.BlockSpec((tm, width), lambda i: (i, 0))
    tab = pl.BlockSpec((tm, hr), lambda i: (i % n_tab, 0))
    headed = lambda width: pl.BlockSpec((N_HEADS, tm, width), lambda i: (0, i, 0))
    return pl.pallas_call(
        _mla_proj_kernel,
        grid=(n // tm,),
        in_specs=[tok(d), tab, tab, _const_spec((1, d)), _const_spec(w["w_in"].shape),
                  _const_spec((1, Q_LORA)), _const_spec((1, KV_LORA)), _const_spec(w["w_qn"].shape),
                  _const_spec(w["w_qa"].shape), _const_spec(w["w_qb"].shape), _const_spec(w["w_uk4"].shape)],
        out_specs=[headed(KV_LORA), headed(QK_ROPE), tok(KV_LORA), tok(QK_ROPE), tok(KV_LORA), tok(QK_ROPE)],
        out_shape=[jax.ShapeDtypeStruct((N_HEADS, n, KV_LORA), q_dtype),
                   jax.ShapeDtypeStruct((N_HEADS, n, QK_ROPE), q_dtype),
                   jax.ShapeDtypeStruct((n, KV_LORA), F32), jax.ShapeDtypeStruct((n, QK_ROPE), F32),
                   jax.ShapeDtypeStruct((n, KV_LORA), BF16), jax.ShapeDtypeStruct((n, QK_ROPE), BF16)],
        compiler_params=_params(("parallel",)),
        name="mla_proj",
    )(x, cos, sin, g.reshape(1, d), w["w_in"], w["g_q"], w["g_kv"], w["w_qn"], w["w_qa"], w["w_qb"], w["w_uk4"])


def _attn_kernel(*refs, has_prefix, tq, tk, n_k):
    if has_prefix:
        q_ref, qr_ref, k_ref, kr_ref, pk_ref, pkr_ref, o_ref, m_ref, l_ref, acc_ref = refs
    else:
        q_ref, qr_ref, k_ref, kr_ref, o_ref, m_ref, l_ref, acc_ref = refs
    qi = pl.program_id(1)
    ki = pl.program_id(2)
    rows = N_HEADS * tq
    last = ((qi + 1) * tq - 1) // tk

    def update(k, kr, masked):
        q = q_ref[...].reshape(rows, KV_LORA)
        qr = qr_ref[...].reshape(rows, QK_ROPE)
        s = _dot_nt(q, k) + _dot_nt(qr, kr)
        if masked:
            r = lax.broadcasted_iota(jnp.int32, s.shape, 0) & (tq - 1)
            c = lax.broadcasted_iota(jnp.int32, s.shape, 1)
            s = jnp.where(c - r <= qi * tq - ki * tk, s, NEG)
        m_old = m_ref[...]
        m_new = jnp.maximum(m_old, jnp.max(s, axis=-1, keepdims=True))
        alpha = jnp.exp(m_old - m_new)
        p = jnp.exp(s - m_new)
        l_ref[...] = alpha * l_ref[...] + jnp.sum(p, axis=-1, keepdims=True)
        acc_ref[...] = alpha * acc_ref[...] + _dot(p.astype(BF16), k)
        m_ref[...] = m_new

    @pl.when(ki == 0)
    def _():
        m_ref[...] = jnp.full(m_ref.shape, NEG, F32)
        l_ref[...] = jnp.zeros(l_ref.shape, F32)
        acc_ref[...] = jnp.zeros(acc_ref.shape, F32)
        if has_prefix:
            update(pk_ref[...], pkr_ref[...], False)

    needs_mask = (ki + 1) * tk - 1 > qi * tq

    @pl.when(jnp.logical_and(ki <= last, needs_mask))
    def _():
        update(k_ref[...], kr_ref[...], True)

    @pl.when(jnp.logical_and(ki <= last, jnp.logical_not(needs_mask)))
    def _():
        update(k_ref[...], kr_ref[...], False)

    @pl.when(ki == n_k - 1)
    def _():
        o = acc_ref[...] / l_ref[...]
        o_ref[...] = o.reshape(N_HEADS, tq, KV_LORA).astype(o_ref.dtype)


def _attn_causal(q_lat, q_pe, ckv_b, kpe_b, prefix, *, batch, tq, tk):
    n = ckv_b.shape[0]
    t_len = n // batch
    assert t_len % tq == 0 and t_len % tk == 0 and tq & (tq - 1) == 0
    n_q, n_k = t_len // tq, t_len // tk
    has_prefix = prefix is not None

    def kmap(b, qi, ki):
        return (b * n_k + jnp.minimum(ki, ((qi + 1) * tq - 1) // tk), 0)

    qmap = lambda b, qi, ki: (0, b * n_q + qi, 0)
    in_specs = [pl.BlockSpec((N_HEADS, tq, KV_LORA), qmap), pl.BlockSpec((N_HEADS, tq, QK_ROPE), qmap),
                pl.BlockSpec((tk, KV_LORA), kmap), pl.BlockSpec((tk, QK_ROPE), kmap)]
    args = [q_lat, q_pe, ckv_b, kpe_b]
    if has_prefix:
        in_specs += [_const_spec(prefix[0].shape), _const_spec(prefix[1].shape)]
        args += list(prefix)
    rows = N_HEADS * tq
    return pl.pallas_call(
        functools.partial(_attn_kernel, has_prefix=has_prefix, tq=tq, tk=tk, n_k=n_k),
        grid=(batch, n_q, n_k),
        in_specs=in_specs,
        out_specs=pl.BlockSpec((N_HEADS, tq, KV_LORA), qmap),
        out_shape=jax.ShapeDtypeStruct((N_HEADS, n, KV_LORA), q_lat.dtype),
        scratch_shapes=[pltpu.VMEM((rows, 1), F32), pltpu.VMEM((rows, 1), F32), pltpu.VMEM((rows, KV_LORA), F32)],
        compiler_params=_params(("parallel", "parallel", "arbitrary")),
        name="attn_causal",
    )(*args)


KEY_CHUNK = 2048


def _attn_decode_kernel(pt_ref, q_ref, qr_ref, ks_ref, krs_ref, cckv_ref, ckpe_ref, o_ref,
                        kbuf, rbuf, sem, *, layer, n_pages, page, s_len):
    b = pl.program_id(0)
    nb = pl.num_programs(0)
    slot = b % 2
    past = n_pages * page

    def copies(bb, sl, p):
        pg = pt_ref[bb, p]
        return (pltpu.make_async_copy(cckv_ref.at[layer, pg], kbuf.at[sl, pl.ds(p * page, page)], sem.at[0, sl]),
                pltpu.make_async_copy(ckpe_ref.at[layer, pg], rbuf.at[sl, pl.ds(p * page, page)], sem.at[1, sl]))

    def fetch(bb, sl):
        def body(p, c):
            for cp in copies(bb, sl, p):
                cp.start()
            return c
        lax.fori_loop(0, n_pages, body, 0)

    @pl.when(b == 0)
    def _():
        kbuf[:, past:, :] = jnp.zeros((2, LANES, KV_LORA), F32)
        rbuf[:, past:, :] = jnp.zeros((2, LANES, QK_ROPE), F32)
        fetch(0, 0)

    @pl.when(b + 1 < nb)
    def _():
        fetch(b + 1, 1 - slot)

    def wait_body(p, c):
        for cp in copies(b, slot, p):
            cp.wait()
        return c
    lax.fori_loop(0, n_pages, wait_body, 0)

    kbuf[slot, past:past + s_len, :] = ks_ref[...]
    rbuf[slot, past:past + s_len, :] = krs_ref[...]

    rows = N_HEADS * s_len
    q = q_ref[...].reshape(rows, KV_LORA).astype(BF16)
    qr = qr_ref[...].reshape(rows, QK_ROPE).astype(BF16)
    bounds = list(range(0, past, KEY_CHUNK)) + [past]
    sizes = [min(KEY_CHUNK, past - c) for c in bounds[:-1]] + [LANES]
    ks, ss = [], []
    for c0, sz in zip(bounds, sizes):
        k = kbuf[slot, c0:c0 + sz, :].astype(BF16)
        kr = rbuf[slot, c0:c0 + sz, :].astype(BF16)
        s = _dot_nt(q, k) + _dot_nt(qr, kr)
        if c0 == past:
            r = lax.broadcasted_iota(jnp.int32, s.shape, 0) & (s_len - 1)
            c = lax.broadcasted_iota(jnp.int32, s.shape, 1)
            s = jnp.where(c <= r, s, NEG)
        ks.append(k)
        ss.append(s)
    m = functools.reduce(jnp.maximum, [jnp.max(s, axis=-1, keepdims=True) for s in ss])
    l = jnp.zeros((rows, 1), F32)
    acc = jnp.zeros((rows, KV_LORA), F32)
    for k, s in zip(ks, ss):
        p = jnp.exp(s - m)
        l = l + jnp.sum(p, axis=-1, keepdims=True)
        acc = acc + _dot(p.astype(BF16), k)
    o_ref[...] = (acc / l).reshape(N_HEADS, s_len, KV_LORA)


def _attn_decode(q_lat, q_pe, ckv_s, kpe_s, cache_ckv, cache_kpe, page_table, *, layer, s_len):
    db, n_pages = page_table.shape
    page = cache_ckv.shape[2]
    assert s_len & (s_len - 1) == 0 and s_len <= LANES
    buf_rows = n_pages * page + LANES
    qmap = lambda b, pt: (0, b, 0)
    kmap = lambda b, pt: (b, 0)
    grid_spec = pltpu.PrefetchScalarGridSpec(
        num_scalar_prefetch=1,
        grid=(db,),
        in_specs=[pl.BlockSpec((N_HEADS, s_len, KV_LORA), qmap), pl.BlockSpec((N_HEADS, s_len, QK_ROPE), qmap),
                  pl.BlockSpec((s_len, KV_LORA), kmap), pl.BlockSpec((s_len, QK_ROPE), kmap),
                  pl.BlockSpec(memory_space=pl.ANY), pl.BlockSpec(memory_space=pl.ANY)],
        out_specs=pl.BlockSpec((N_HEADS, s_len, KV_LORA), qmap),
        scratch_shapes=[pltpu.VMEM((2, buf_rows, KV_LORA), F32), pltpu.VMEM((2, buf_rows, QK_ROPE), F32),
                        pltpu.SemaphoreType.DMA((2, 2))],
    )
    return pl.pallas_call(
        functools.partial(_attn_decode_kernel, layer=layer, n_pages=n_pages, page=page, s_len=s_len),
        grid_spec=grid_spec,
        out_shape=jax.ShapeDtypeStruct((N_HEADS, db * s_len, KV_LORA), F32),
        compiler_params=_params(("arbitrary",)),
        name="attn_decode",
    )(page_table, q_lat, q_pe, ckv_s, kpe_s, cache_ckv, cache_kpe)


def _attn_out_kernel(o_ref, x_ref, wuv_ref, wo_ref, out_ref):
    pieces = []
    for p in range(N_HEADS // 2):
        pieces.append(_dot(o_ref[2 * p].astype(BF16), wuv_ref[2 * p])
                      + _dot(o_ref[2 * p + 1].astype(BF16), wuv_ref[2 * p + 1]))
    o = jnp.concatenate(pieces, axis=-1).astype(BF16)
    out_ref[...] = x_ref[...] + _dot(o, wo_ref[...])


def _attn_out(o_lat, x, w_uv2, w_o, *, tm):
    n, d = x.shape
    assert n % tm == 0
    return pl.pallas_call(
        _attn_out_kernel,
        grid=(n // tm,),
        in_specs=[pl.BlockSpec((N_HEADS, tm, KV_LORA), lambda i: (0, i, 0)), pl.BlockSpec((tm, d), lambda i: (i, 0)),
                  _const_spec(w_uv2.shape), _const_spec(w_o.shape)],
        out_specs=pl.BlockSpec((tm, d), lambda i: (i, 0)),
        out_shape=jax.ShapeDtypeStruct((n, d), F32),
        compiler_params=_params(("parallel",)),
        name="attn_out",
    )(o_lat, x, w_uv2, w_o)


def _router_kernel(x_ref, g_ref, whi_ref, wlo_ref, b_ref, h_ref, idx_ref, gate_ref):
    h = _rms(x_ref[...], g_ref[...])
    h_ref[...] = h
    h_hi = h.astype(BF16)
    h_lo = (h - h_hi.astype(F32)).astype(BF16)
    logits = _dot(h_hi, whi_ref[...]) + _dot(h_lo, whi_ref[...]) + _dot(h_hi, wlo_ref[...])
    lane = lax.broadcasted_iota(jnp.int32, logits.shape, 1)
    lane_f = lane.astype(F32)
    sel = jnp.where(lane < N_EXPERTS, logits + b_ref[...], NEG)
    m1 = jnp.max(sel, axis=-1, keepdims=True)
    i1 = jnp.min(jnp.where(sel == m1, lane_f, float(LANES)), axis=-1, keepdims=True)
    sel2 = jnp.where(lane_f == i1, NEG, sel)
    m2 = jnp.max(sel2, axis=-1, keepdims=True)
    i2 = jnp.min(jnp.where(sel2 == m2, lane_f, float(LANES)), axis=-1, keepdims=True)
    l1 = jnp.sum(jnp.where(lane_f == i1, logits, 0.0), axis=-1, keepdims=True)
    l2 = jnp.sum(jnp.where(lane_f == i2, logits, 0.0), axis=-1, keepdims=True)
    mx = jnp.maximum(l1, l2)
    e1 = jnp.exp(l1 - mx)
    e2 = jnp.exp(l2 - mx)
    den = e1 + e2
    idx_ref[...] = jnp.where(lane == 0, i1, jnp.where(lane == 1, i2, 0.0)).astype(jnp.int32)
    gate_ref[...] = jnp.where(lane == 0, e1 / den, jnp.where(lane == 1, e2 / den, 0.0))


def _router(x, g, w_hi, w_lo, bias, *, tm):
    n, d = x.shape
    assert n % tm == 0
    tok = lambda width: pl.BlockSpec((tm, width), lambda i: (i, 0))
    return pl.pallas_call(
        _router_kernel,
        grid=(n // tm,),
        in_specs=[tok(d), _const_spec((1, d)), _const_spec(w_hi.shape), _const_spec(w_lo.shape),
                  _const_spec((1, LANES))],
        out_specs=[tok(d), tok(LANES), tok(LANES)],
        out_shape=[jax.ShapeDtypeStruct((n, d), F32), jax.ShapeDtypeStruct((n, LANES), jnp.int32),
                   jax.ShapeDtypeStruct((n, LANES), F32)],
        compiler_params=_params(("parallel",)),
        name="moe_router",
    )(x, g.reshape(1, d), w_hi, w_lo, bias)


def _dispatch_kernel(pos_ref, h_ref, init_ref, out_ref, sem, *, tm):
    del init_ref
    base = pl.program_id(0) * tm

    def copy(r, k):
        dst = pos_ref[TOP_K * (base + r) + k]
        return pltpu.make_async_copy(h_ref.at[pl.ds(r, 1)], out_ref.at[pl.ds(dst, 1)], sem)

    def start(r, c):
        for k in range(TOP_K):
            copy(r, k).start()
        return c

    def wait(r, c):
        for k in range(TOP_K):
            copy(r, k).wait()
        return c

    lax.fori_loop(0, tm, start, 0)
    lax.fori_loop(0, tm, wait, 0)


def _dispatch(pos, h, n_rows, *, tm):
    n, d = h.shape
    assert n % tm == 0
    grid_spec = pltpu.PrefetchScalarGridSpec(
        num_scalar_prefetch=1,
        grid=(n // tm,),
        in_specs=[pl.BlockSpec((tm, d), lambda i, pos: (i, 0)), pl.BlockSpec(memory_space=pl.ANY)],
        out_specs=pl.BlockSpec(memory_space=pl.ANY),
        scratch_shapes=[pltpu.SemaphoreType.DMA(())],
    )
    return pl.pallas_call(
        functools.partial(_dispatch_kernel, tm=tm),
        grid_spec=grid_spec,
        out_shape=jax.ShapeDtypeStruct((n_rows, d), F32),
        input_output_aliases={2: 0},
        compiler_params=_params(("arbitrary",)),
        name="moe_dispatch",
    )(pos, h, jnp.zeros((n_rows, d), F32))


def _expert_kernel(te_ref, x_ref, wg_ref, wu_ref, wd_ref, o_ref, acc_ref, *, n_tiles):
    t = pl.program_id(0)

    @pl.when(t < te_ref[n_tiles])
    def _():
        _swiglu_into(acc_ref, x_ref[...].astype(BF16), wg_ref, wu_ref, wd_ref, lead=(0,))
        o_ref[...] = acc_ref[...]

    @pl.when(t >= te_ref[n_tiles])
    def _():
        o_ref[...] = jnp.zeros(o_ref.shape, F32)


def _expert_ffn(te, xs, wg, wu, wd, *, tm):
    r, d = xs.shape
    n_tiles = r // tm
    wmap = lambda t, te: (te[t], 0, 0)
    grid_spec = pltpu.PrefetchScalarGridSpec(
        num_scalar_prefetch=1,
        grid=(n_tiles,),
        in_specs=[pl.BlockSpec((tm, d), lambda t, te: (t, 0)),
                  pl.BlockSpec((1,) + wg.shape[1:], wmap), pl.BlockSpec((1,) + wu.shape[1:], wmap),
                  pl.BlockSpec((1,) + wd.shape[1:], wmap)],
        out_specs=pl.BlockSpec((tm, d), lambda t, te: (t, 0)),
        scratch_shapes=[pltpu.VMEM((tm, d), F32)],
    )
    return pl.pallas_call(
        functools.partial(_expert_kernel, n_tiles=n_tiles),
        grid_spec=grid_spec,
        out_shape=jax.ShapeDtypeStruct((r, d), F32),
        compiler_params=_params(("arbitrary",)),
        name="moe_experts",
    )(te, xs, wg, wu, wd)


def _combine_kernel(pos_ref, x_ref, gate_ref, gf_ref, y_ref, o_ref, buf, sem, *, tm, final_norm):
    base = pl.program_id(0) * tm

    def copy(r, k):
        src = pos_ref[TOP_K * (base + r) + k]
        return pltpu.make_async_copy(y_ref.at[pl.ds(src, 1)], buf.at[k, pl.ds(r, 1)], sem)

    def start(r, c):
        for k in range(TOP_K):
            copy(r, k).start()
        return c

    def wait(r, c):
        for k in range(TOP_K):
            copy(r, k).wait()
        return c

    lax.fori_loop(0, tm, start, 0)
    lax.fori_loop(0, tm, wait, 0)
    gate = gate_ref[...]
    out = x_ref[...] + gate[:, 0:1] * buf[0] + gate[:, 1:2] * buf[1]
    if final_norm:
        out = _rms(out, gf_ref[...])
    o_ref[...] = out


def _combine(pos, x, gates, g_final, y, *, tm, final_norm):
    n, d = x.shape
    assert n % tm == 0
    grid_spec = pltpu.PrefetchScalarGridSpec(
        num_scalar_prefetch=1,
        grid=(n // tm,),
        in_specs=[pl.BlockSpec((tm, d), lambda i, pos: (i, 0)), pl.BlockSpec((tm, LANES), lambda i, pos: (i, 0)),
                  pl.BlockSpec((1, d), lambda i, pos: (0, 0)), pl.BlockSpec(memory_space=pl.ANY)],
        out_specs=pl.BlockSpec((tm, d), lambda i, pos: (i, 0)),
        scratch_shapes=[pltpu.VMEM((TOP_K, tm, d), F32), pltpu.SemaphoreType.DMA(())],
    )
    return pl.pallas_call(
        functools.partial(_combine_kernel, tm=tm, final_norm=final_norm),
        grid_spec=grid_spec,
        out_shape=jax.ShapeDtypeStruct((n, d), F32),
        compiler_params=_params(("arbitrary",)),
        name="moe_combine",
    )(pos, x, gates, g_final.reshape(1, d), y)


def _route_positions(idx, tm, n_tiles):
    e = idx[:, :TOP_K].reshape(-1)
    onehot = (e[:, None] == jnp.arange(N_EXPERTS, dtype=jnp.int32)[None, :]).astype(jnp.int32)
    csum = jnp.cumsum(onehot, axis=0)
    rank = jnp.sum(csum * onehot, axis=1) - 1
    counts = csum[-1]
    padded = (counts + tm - 1) // tm * tm
    ends = jnp.cumsum(padded)
    pos = (ends - padded)[e] + rank
    n_used = ends[-1] // tm
    starts = jnp.arange(n_tiles, dtype=jnp.int32) * tm
    te = jnp.sum((starts[:, None] >= ends[None, :]).astype(jnp.int32), axis=1)
    te = jnp.minimum(te, jnp.take(te, jnp.maximum(n_used - 1, 0)))
    return pos.astype(jnp.int32), jnp.concatenate([te, n_used[None]]).astype(jnp.int32)


def _moe(x, g, w, g_final, *, tm, tile, final_norm):
    n, _ = x.shape
    h, idx, gates = _router(x, g, w["r_hi"], w["r_lo"], w["r_b"], tm=tm)
    n_tiles = -(-TOP_K * n // tile) + N_EXPERTS
    pos, te = _route_positions(idx, tile, n_tiles)
    xs = _dispatch(pos, h, n_tiles * tile, tm=tm)
    ys = _expert_ffn(te, xs, w["wg"], w["wu"], w["wd"], tm=tile)
    return _combine(pos, x, gates, g_final, ys, tm=tm, final_norm=final_norm)


def _rope_tables(pos, rows):
    half = QK_ROPE // 2
    freqs = 1.0 / (ROPE_BASE ** (jnp.arange(half, dtype=F32) * (2.0 / QK_ROPE)))
    ang = pos.astype(F32)[:, None] * freqs[None, :]
    cos, sin = jnp.cos(ang), jnp.sin(ang)
    cos = jnp.tile(jnp.concatenate([cos, cos], axis=-1), (rows // pos.shape[0], N_HEADS))
    sin = jnp.tile(jnp.concatenate([-sin, sin], axis=-1), (rows // pos.shape[0], N_HEADS))
    return cos, sin


def _mla_weights(w_in, g_q, g_kv, w_uq, w_uk, w_uv, w_o):
    half = QK_ROPE // 2
    swap = jnp.concatenate([jnp.arange(half, QK_ROPE), jnp.arange(half)])
    c2 = Q_LORA + KV_LORA
    w_in_ext = jnp.concatenate([w_in, w_in[:, c2:][:, swap]], axis=1)
    uq = w_uq.reshape(Q_LORA, N_HEADS, QK_NOPE + QK_ROPE)
    rope_cols = uq[:, :, QK_NOPE:]
    a = jnp.transpose(w_uk, (1, 2, 0)).reshape(N_HEADS // 4, 4, QK_NOPE, KV_LORA)
    w_uk4 = jnp.einsum("pjnc,jk->pjnkc", a, jnp.eye(4, dtype=a.dtype)).reshape(N_HEADS // 4, 4 * QK_NOPE, 4 * KV_LORA)
    side = jax.nn.one_hot(jnp.arange(N_HEADS) % 2, 2, dtype=w_uv.dtype)
    w_uv2 = jnp.einsum("chv,hk->hckv", w_uv, side).reshape(N_HEADS, KV_LORA, 2 * V_HEAD)
    return {
        "w_in": w_in_ext.astype(BF16), "g_q": g_q.reshape(1, -1), "g_kv": g_kv.reshape(1, -1),
        "w_qn": uq[:, :, :QK_NOPE].reshape(Q_LORA, -1).astype(BF16),
        "w_qa": rope_cols.reshape(Q_LORA, -1).astype(BF16),
        "w_qb": rope_cols[:, :, swap].reshape(Q_LORA, -1).astype(BF16),
        "w_uk4": w_uk4.astype(BF16), "w_uv2": w_uv2.astype(BF16), "w_o": w_o.astype(BF16),
    }


def _moe_weights(w_router, b_router, w_gate, w_up, w_down):
    d = w_router.shape[0]
    wr = jnp.zeros((d, LANES), F32).at[:, :N_EXPERTS].set(w_router)
    hi = wr.astype(BF16)
    return {
        "r_hi": hi, "r_lo": (wr - hi.astype(F32)).astype(BF16),
        "r_b": jnp.zeros((1, LANES), F32).at[0, :N_EXPERTS].set(b_router),
        "wg": w_gate.astype(BF16), "wu": w_up.astype(BF16), "wd": w_down.astype(BF16),
    }


def _tile(n, want):
    if n <= want:
        return n
    t = want - want % 16
    while n % t:
        t -= 16
    return t


def kernel(x_prompt, x_sample, cache_ckv, cache_kpe, state_pool, page_table, meta_tokens, g_mix, g_ffn, g_final, pool_w, pool_scale, mla_w_in, mla_g_q, mla_g_kv, mla_w_uq, mla_w_uk, mla_w_uv, mla_w_o, ffn_w_gate, ffn_w_up, ffn_w_down, moe_w_router, moe_b_router, moe_w_gate, moe_w_up, moe_w_down):
    b, seq, d = x_prompt.shape
    db, s_len, _ = x_sample.shape
    depth = g_mix.shape[0]
    assert depth % 2 == 0
    past_len = page_table.shape[1] * cache_ckv.shape[2]
    n_main, n_samp = b * seq, db * s_len

    xm = x_prompt
    xs = x_sample
    xe = meta_tokens[None].astype(F32)

    tm_main = _tile(n_main, 1024)
    tm_samp = _tile(n_samp, 512)
    tq = min(128, seq)
    tk = min(512, seq)
    cos_m, sin_m = _rope_tables(N_META + jnp.arange(seq), max(seq, tm_main))
    cos_s, sin_s = _rope_tables(past_len + jnp.arange(s_len), tm_samp)
    cos_e, sin_e = _rope_tables(jnp.arange(N_META), N_META)

    outs = {k: [] for k in ("ckv_p", "kpe_p", "pool_p", "ckv_s", "kpe_s", "pool_s")}
    for i in range(depth):
        j = i // 2
        last = i == depth - 1
        if i % 2 == 0:
            w_bf = pool_w[j].astype(BF16)
            zeros_e = jnp.zeros((1, HIST, d), F32)
            xe, tail_e = _pool_layer(xe, zeros_e, g_mix[i], w_bf, pool_scale[j], has_hist=False, tb=1, tl=N_META)
            xm, tail_m = _pool_layer(xm, tail_e, g_mix[i], w_bf, pool_scale[j], has_hist=True, tb=1,
                                     tl=min(512, seq))
            prev_s = jnp.concatenate([jnp.zeros((db, 1, d), F32), state_pool[j]], axis=1)
            xs, tail_s = _pool_layer(xs, prev_s, g_mix[i], w_bf, pool_scale[j], has_hist=True,
                                     tb=_tile(db, 64), tl=s_len)
            outs["pool_p"].append(tail_m[:, 1:])
            outs["pool_s"].append(tail_s[:, 1:])

            wg, wu, wd = ffn_w_gate[j].astype(BF16), ffn_w_up[j].astype(BF16), ffn_w_down[j].astype(BF16)
            xm = _ffn_dense(xm.reshape(n_main, d), g_ffn[i], wg, wu, wd, tm=tm_main).reshape(b, seq, d)
            small = jnp.concatenate([xs.reshape(n_samp, d), xe[0]], axis=0)
            small = _ffn_dense(small, g_ffn[i], wg, wu, wd, tm=small.shape[0])
            xs, xe = small[:n_samp].reshape(db, s_len, d), small[n_samp:][None]
        else:
            w = _mla_weights(mla_w_in[j], mla_g_q[j], mla_g_kv[j], mla_w_uq[j], mla_w_uk[j], mla_w_uv[j], mla_w_o[j])
            xe2, xm2, xs2 = xe[0], xm.reshape(n_main, d), xs.reshape(n_samp, d)
            ql_e, qp_e, ckv_e, kpe_e, ckvb_e, kpeb_e = _mla_proj(xe2, cos_e, sin_e, g_mix[i], w, tm=N_META, q_dtype=BF16)
            ql_m, qp_m, ckv_m, kpe_m, ckvb_m, kpeb_m = _mla_proj(xm2, cos_m, sin_m, g_mix[i], w, tm=tm_main, q_dtype=BF16)
            ql_s, qp_s, ckv_s, kpe_s, _, _ = _mla_proj(xs2, cos_s, sin_s, g_mix[i], w, tm=tm_samp, q_dtype=F32)

            o_e = _attn_causal(ql_e, qp_e, ckvb_e, kpeb_e, None, batch=1, tq=N_META, tk=N_META)
            o_m = _attn_causal(ql_m, qp_m, ckvb_m, kpeb_m, (ckvb_e, kpeb_e), batch=b, tq=tq, tk=tk)
            o_s = _attn_decode(ql_s, qp_s, ckv_s, kpe_s, cache_ckv, cache_kpe, page_table, layer=j, s_len=s_len)

            xe2 = _attn_out(o_e, xe2, w["w_uv2"], w["w_o"], tm=N_META)
            xm2 = _attn_out(o_m, xm2, w["w_uv2"], w["w_o"], tm=tm_main)
            xs2 = _attn_out(o_s, xs2, w["w_uv2"], w["w_o"], tm=tm_samp)

            outs["ckv_p"].append(jnp.concatenate(
                [jnp.broadcast_to(ckv_e[None], (b, N_META, KV_LORA)), ckv_m.reshape(b, seq, KV_LORA)], axis=1))
            outs["kpe_p"].append(jnp.concatenate(
                [jnp.broadcast_to(kpe_e[None], (b, N_META, QK_ROPE)), kpe_m.reshape(b, seq, QK_ROPE)], axis=1))
            outs["ckv_s"].append(ckv_s.reshape(db, s_len, KV_LORA))
            outs["kpe_s"].append(kpe_s.reshape(db, s_len, QK_ROPE))

            mw = _moe_weights(moe_w_router[j], moe_b_router[j], moe_w_gate[j], moe_w_up[j], moe_w_down[j])
            xm = _moe(xm2, g_ffn[i], mw, g_final, tm=_tile(n_main, 512), tile=512, final_norm=last).reshape(b, seq, d)
            small = jnp.concatenate([xs2, xe2], axis=0)
            small = _moe(small, g_ffn[i], mw, g_final, tm=small.shape[0], tile=128, final_norm=last)
            xs, xe = small[:n_samp].reshape(db, s_len, d), small[n_samp:][None]

    return (xm, xs, jnp.stack(outs["ckv_p"]), jnp.stack(outs["kpe_p"]), jnp.stack(outs["pool_p"]),
            jnp.stack(outs["ckv_s"]), jnp.stack(outs["kpe_s"]), jnp.stack(outs["pool_s"]))
```

```python
import functools

import jax
import jax.numpy as jnp
from jax import lax
from jax.experimental import pallas as pl
from jax.experimental.pallas import tpu as pltpu

EPS = 1e-6
N_META = 16
POOL_WINDOWS = (2, 4, 8, 16)
N_HEADS = 16
Q_LORA = 384
KV_LORA = 256
QK_NOPE = 64
QK_ROPE = 32
V_HEAD = 64
ROPE_BASE = 10000.0
ATTN_SCALE = (QK_NOPE + QK_ROPE) ** -0.5
Q_SCALE = ATTN_SCALE * 1.4426950408889634
N_EXPERTS = 8
TOP_K = 2

HIST = 16
LANES = 128
NEG = -1e30
VMEM_LIMIT = 56 * 1024 * 1024
BF16 = jnp.bfloat16
F32 = jnp.float32


def _dot(a, b):
    return jnp.dot(a, b, preferred_element_type=F32)


def _dot_nt(a, b):
    return lax.dot_general(a, b, (((1,), (1,)), ((), ())), preferred_element_type=F32)


def _rms(x, g):
    return x * lax.rsqrt(jnp.mean(x * x, axis=-1, keepdims=True) + EPS) * g


def _params(sem, vmem=VMEM_LIMIT):
    return pltpu.CompilerParams(dimension_semantics=sem, vmem_limit_bytes=vmem)


def _const_spec(shape):
    nd = len(shape)
    return pl.BlockSpec(shape, lambda *_: (0,) * nd)


def _pool_kernel(*refs, has_hist, multi_tile, tb, tl):
    if multi_tile:
        x_ref, halo_ref, prev_ref, g_ref, w_ref, sc_ref, o_ref, tail_ref, hext_ref = refs
    else:
        x_ref, prev_ref, g_ref, w_ref, sc_ref, o_ref, tail_ref, hext_ref = refs
        halo_ref = None
    t = pl.program_id(1)
    d = x_ref.shape[-1]
    gc = d // len(POOL_WINDOWS)
    g = g_ref[...]
    x = x_ref[...]
    h = _rms(x, g)
    hext_ref[:, HIST:, :] = h
    prev = jnp.broadcast_to(prev_ref[...], (tb, HIST, d))
    if multi_tile:
        @pl.when(t == 0)
        def _():
            hext_ref[:, :HIST, :] = prev

        @pl.when(t > 0)
        def _():
            hext_ref[:, :HIST, :] = _rms(halo_ref[...], g)
    else:
        hext_ref[:, :HIST, :] = prev

    for gi, w in enumerate(POOL_WINDOWS):
        c0, c1 = gi * gc, (gi + 1) * gc
        win = hext_ref[:, pl.ds(HIST, tl), c0:c1]
        for i in range(1, w):
            win = win + hext_ref[:, pl.ds(HIST - i, tl), c0:c1]
        hg = hext_ref[:, pl.ds(HIST, tl), c0:c1]
        if has_hist:
            dg = win * (1.0 / w) - hg
        else:
            row = lax.broadcasted_iota(jnp.int32, (tb, tl, gc), 1) + t * tl
            cnt = jnp.minimum(row + 1, w).astype(F32)
            dg = win / cnt - hg
        dg = dg.reshape(tb * tl, gc).astype(BF16)
        yg = _dot(dg, w_ref[gi]).reshape(tb, tl, gc)
        o_ref[:, :, c0:c1] = x_ref[:, :, c0:c1] + yg * sc_ref[:, c0:c1]
    tail_ref[...] = hext_ref[:, pl.ds(tl, HIST), :]


def _pool_layer(x, prev, g, w_bf, scale, *, has_hist, tb, tl):
    b, l, d = x.shape
    n_t = l // tl
    multi = n_t > 1
    assert l % tl == 0 and b % tb == 0 and (not multi or (tb == 1 and tl % HIST == 0))
    shared = prev.shape[0] == 1
    per_tile = tl // HIST
    in_specs = [pl.BlockSpec((tb, tl, d), lambda i, t: (i, t, 0))]
    args = [x]
    if multi:
        in_specs.append(pl.BlockSpec((tb, HIST, d), lambda i, t: (i, jnp.maximum(t * per_tile - 1, 0), 0)))
        args.append(x)
    in_specs += [
        pl.BlockSpec((1 if shared else tb, HIST, d), (lambda i, t: (0, 0, 0)) if shared else (lambda i, t: (i, 0, 0))),
        _const_spec((1, d)), _const_spec(w_bf.shape), _const_spec((1, d)),
    ]
    args += [prev, g.reshape(1, d), w_bf, scale.reshape(1, d)]
    return pl.pallas_call(
        functools.partial(_pool_kernel, has_hist=has_hist, multi_tile=multi, tb=tb, tl=tl),
        grid=(b // tb, n_t),
        in_specs=in_specs,
        out_specs=[pl.BlockSpec((tb, tl, d), lambda i, t: (i, t, 0)),
                   pl.BlockSpec((tb, HIST, d), lambda i, t: (i, 0, 0))],
        out_shape=[jax.ShapeDtypeStruct((b, l, d), F32), jax.ShapeDtypeStruct((b, HIST, d), F32)],
        scratch_shapes=[pltpu.VMEM((tb, tl + HIST, d), F32)],
        compiler_params=_params(("parallel", "arbitrary")),
        name="pool_mixer",
    )(*args)


FF_CHUNK = 256


def _swiglu_into(acc_ref, h, wg_ref, wu_ref, wd_ref, lead=()):
    n_ff = wg_ref.shape[-1]
    for c in range(0, n_ff, FF_CHUNK):
        gg = _dot(h, wg_ref[lead + (slice(None), slice(c, c + FF_CHUNK))])
        uu = _dot(h, wu_ref[lead + (slice(None), slice(c, c + FF_CHUNK))])
        a = (gg * jax.nn.sigmoid(gg) * uu).astype(BF16)
        y = _dot(a, wd_ref[lead + (slice(c, c + FF_CHUNK), slice(None))])
        if c == 0:
            acc_ref[...] = y
        else:
            acc_ref[...] += y


def _ffn_kernel(x_ref, g_ref, wg_ref, wu_ref, wd_ref, o_ref, acc_ref):
    x = x_ref[...]
    h = _rms(x, g_ref[...]).astype(BF16)
    _swiglu_into(acc_ref, h, wg_ref, wu_ref, wd_ref)
    o_ref[...] = x + acc_ref[...]


def _ffn_dense(x, g, wg, wu, wd, *, tm):
    n, d = x.shape
    assert n % tm == 0
    resident = lambda shape: pl.BlockSpec(shape, lambda i: (0, 0), pipeline_mode=pl.Buffered(1))
    return pl.pallas_call(
        _ffn_kernel,
        grid=(n // tm,),
        in_specs=[pl.BlockSpec((tm, d), lambda i: (i, 0)), _const_spec((1, d)),
                  resident(wg.shape), resident(wu.shape), resident(wd.shape)],
        out_specs=pl.BlockSpec((tm, d), lambda i: (i, 0)),
        out_shape=jax.ShapeDtypeStruct((n, d), F32),
        scratch_shapes=[pltpu.VMEM((tm, d), F32)],
        compiler_params=_params(("parallel",)),
        name="ffn_dense",
    )(x, g.reshape(1, d), wg, wu, wd)


def _mla_proj_kernel(x_ref, cos_ref, sin_ref, g_ref, win_ref, gq_ref, gkv_ref, wqn_ref, wqa_ref, wqb_ref, wuk_ref,
                     qlat_ref, qpe_ref, ckv_ref, kpe_ref, ckvb_ref, kpeb_ref):
    h = _rms(x_ref[...], g_ref[...]).astype(BF16)
    proj = _dot(h, win_ref[...])
    c1, c2, c3 = Q_LORA, Q_LORA + KV_LORA, Q_LORA + KV_LORA + QK_ROPE
    cqn = _rms(proj[:, :c1], gq_ref[...]).astype(BF16)
    ckv = _rms(proj[:, c1:c2], gkv_ref[...])
    cos = cos_ref[...]
    sin = sin_ref[...]
    kpe = proj[:, c2:c3] * cos[:, :QK_ROPE] + proj[:, c3:c3 + QK_ROPE] * sin[:, :QK_ROPE]
    ckv_ref[...] = ckv
    kpe_ref[...] = kpe
    ckvb_ref[...] = ckv.astype(BF16)
    kpeb_ref[...] = kpe.astype(BF16)

    qn = _dot(cqn, wqn_ref[...]).astype(BF16)
    quad = 4 * QK_NOPE
    for p in range(N_HEADS // 4):
        ql = _dot(qn[:, p * quad:(p + 1) * quad], wuk_ref[p]) * Q_SCALE
        for j in range(4):
            qlat_ref[4 * p + j] = ql[:, j * KV_LORA:(j + 1) * KV_LORA].astype(qlat_ref.dtype)
    qr = (_dot(cqn, wqa_ref[...]) * cos + _dot(cqn, wqb_ref[...]) * sin) * Q_SCALE
    for hd in range(N_HEADS):
        qpe_ref[hd] = qr[:, hd * QK_ROPE:(hd + 1) * QK_ROPE].astype(qpe_ref.dtype)


def _mla_proj(x, cos, sin, g, w, *, tm, q_dtype):
    n, d = x.shape
    assert n % tm == 0 and cos.shape[0] % tm == 0
    n_tab = cos.shape[0] // tm
    hr = N_HEADS * QK_ROPE
    tok = lambda width: pl.BlockSpec((tm, width), lambda i: (i, 0))
    tab = pl.BlockSpec((tm, hr), lambda i: (i % n_tab, 0))
    headed = lambda width: pl.BlockSpec((N_HEADS, tm, width), lambda i: (0, i, 0))
    return pl.pallas_call(
        _mla_proj_kernel,
        grid=(n // tm,),
        in_specs=[tok(d), tab, tab, _const_spec((1, d)), _const_spec(w["w_in"].shape),
                  _const_spec((1, Q_LORA)), _const_spec((1, KV_LORA)), _const_spec(w["w_qn"].shape),
                  _const_spec(w["w_qa"].shape), _const_spec(w["w_qb"].shape), _const_spec(w["w_uk4"].shape)],
        out_specs=[headed(KV_LORA), headed(QK_ROPE), tok(KV_LORA), tok(QK_ROPE), tok(KV_LORA), tok(QK_ROPE)],
        out_shape=[jax.ShapeDtypeStruct((N_HEADS, n, KV_LORA), q_dtype),
                   jax.ShapeDtypeStruct((N_HEADS, n, QK_ROPE), q_dtype),
                   jax.ShapeDtypeStruct((n, KV_LORA), F32), jax.ShapeDtypeStruct((n, QK_ROPE), F32),
                   jax.ShapeDtypeStruct((n, KV_LORA), BF16), jax.ShapeDtypeStruct((n, QK_ROPE), BF16)],
        compiler_params=_params(("parallel",)),
        name="mla_proj",
    )(x, cos, sin, g.reshape(1, d), w["w_in"], w["g_q"], w["g_kv"], w["w_qn"], w["w_qa"], w["w_qb"], w["w_uk4"])


ATTN_SPLIT = 2


def _attn_kernel(*refs, n_pre, t_len, tqb, tqs, tk):
    if n_pre:
        q_ref, qr_ref, k_ref, kr_ref, pk_ref, pkr_ref, o_ref, kf, krf, m_ref, l_ref, acc_ref = refs
    else:
        q_ref, qr_ref, k_ref, kr_ref, o_ref, kf, krf, m_ref, l_ref, acc_ref = refs
    qi = pl.program_id(1)
    n_keys = n_pre + t_len
    hg = N_HEADS // ATTN_SPLIT
    rows = hg * tqs
    tk_shift = tk.bit_length() - 1

    @pl.when(qi == 0)
    def _():
        if n_pre:
            kf[0:n_pre, :] = pk_ref[...]
            krf[0:n_pre, :] = pkr_ref[...]
        kf[n_pre:n_keys, :] = k_ref[...]
        krf[n_pre:n_keys, :] = kr_ref[...]
        if kf.shape[0] > n_keys:
            kf[n_keys:, :] = jnp.zeros((kf.shape[0] - n_keys, KV_LORA), kf.dtype)
            krf[n_keys:, :] = jnp.zeros((kf.shape[0] - n_keys, QK_ROPE), krf.dtype)

    def sub_block(i, carry):
        t0 = pl.multiple_of(i * tqs, tqs)
        p0 = n_pre + qi * tqb + i * tqs
        for g in range(ATTN_SPLIT):
            m_ref[g] = jnp.full((rows, 1), NEG, F32)
            l_ref[g] = jnp.zeros((rows, 1), F32)
            acc_ref[g] = jnp.zeros((rows, KV_LORA), F32)

        def update(j, masked):
            k0 = pl.multiple_of(j * tk, tk)
            k = kf[pl.ds(k0, tk), :]
            kr = krf[pl.ds(k0, tk), :]
            for g in range(ATTN_SPLIT):
                q = q_ref[g * hg:(g + 1) * hg, pl.ds(t0, tqs), :].reshape(rows, KV_LORA)
                qr = qr_ref[g * hg:(g + 1) * hg, pl.ds(t0, tqs), :].reshape(rows, QK_ROPE)
                s = _dot_nt(q, k) + _dot_nt(qr, kr)
                if masked:
                    r = lax.broadcasted_iota(jnp.int32, s.shape, 0) & (tqs - 1)
                    c = lax.broadcasted_iota(jnp.int32, s.shape, 1)
                    s = jnp.where(c - r <= p0 - j * tk, s, NEG)
                m_old = m_ref[g]
                m_new = jnp.maximum(m_old, jnp.max(s, axis=-1, keepdims=True))
                alpha = jnp.exp2(m_old - m_new)
                p = jnp.exp2(s - m_new)
                l_ref[g] = alpha * l_ref[g] + jnp.sum(p, axis=-1, keepdims=True)
                acc_ref[g] = alpha * acc_ref[g] + _dot(p.astype(BF16), k)
                m_ref[g] = m_new

        def full_tile(j, c):
            update(j, False)
            return c

        def edge_tile(j, c):
            update(j, True)
            return c

        n_full = (p0 + 1) >> tk_shift
        last = (p0 + tqs - 1) >> tk_shift
        lax.fori_loop(0, n_full, full_tile, 0)
        lax.fori_loop(n_full, last + 1, edge_tile, 0)
        for g in range(ATTN_SPLIT):
            o = acc_ref[g] / l_ref[g]
            o_ref[g * hg:(g + 1) * hg, pl.ds(t0, tqs), :] = o.reshape(hg, tqs, KV_LORA).astype(o_ref.dtype)
        return carry

    lax.fori_loop(0, tqb // tqs, sub_block, 0)


def _attn_causal(q_lat, q_pe, ckv_b, kpe_b, prefix, *, batch, tqb, tqs, tk):
    n = ckv_b.shape[0]
    t_len = n // batch
    n_pre = 0 if prefix is None else prefix[0].shape[0]
    assert t_len % tqb == 0 and tqb % tqs == 0 and tk % tqs == 0 and tk & (tk - 1) == 0 and tqs & (tqs - 1) == 0
    assert tqs % 16 == 0 and n_pre % 16 == 0 and (n_pre + t_len) % 16 == 0
    n_q = t_len // tqb
    key_rows = -(-(n_pre + t_len) // tk) * tk
    qmap = lambda b, qi: (0, b * n_q + qi, 0)
    kmap = lambda b, qi: (b, 0)
    in_specs = [pl.BlockSpec((N_HEADS, tqb, KV_LORA), qmap), pl.BlockSpec((N_HEADS, tqb, QK_ROPE), qmap),
                pl.BlockSpec((t_len, KV_LORA), kmap), pl.BlockSpec((t_len, QK_ROPE), kmap)]
    args = [q_lat, q_pe, ckv_b, kpe_b]
    if n_pre:
        in_specs += [_const_spec(prefix[0].shape), _const_spec(prefix[1].shape)]
        args += list(prefix)
    rows = N_HEADS // ATTN_SPLIT * tqs
    return pl.pallas_call(
        functools.partial(_attn_kernel, n_pre=n_pre, t_len=t_len, tqb=tqb, tqs=tqs, tk=tk),
        grid=(batch, n_q),
        in_specs=in_specs,
        out_specs=pl.BlockSpec((N_HEADS, tqb, KV_LORA), qmap),
        out_shape=jax.ShapeDtypeStruct((N_HEADS, n, KV_LORA), q_lat.dtype),
        scratch_shapes=[pltpu.VMEM((key_rows, KV_LORA), BF16), pltpu.VMEM((key_rows, QK_ROPE), BF16),
                        pltpu.VMEM((ATTN_SPLIT, rows, 1), F32), pltpu.VMEM((ATTN_SPLIT, rows, 1), F32),
                        pltpu.VMEM((ATTN_SPLIT, rows, KV_LORA), F32)],
        compiler_params=_params(("parallel", "arbitrary")),
        name="attn_causal",
    )(*args)


KEY_CHUNK = 2048


def _attn_decode_kernel(pt_ref, q_ref, qr_ref, ks_ref, krs_ref, cckv_ref, ckpe_ref, o_ref,
                        kbuf, rbuf, sem, *, layer, n_pages, page, s_len):
    b = pl.program_id(0)
    nb = pl.num_programs(0)
    slot = b % 2
    past = n_pages * page

    def copies(bb, sl, p):
        pg = pt_ref[bb, p]
        off = pl.multiple_of(p * page, page)
        return (pltpu.make_async_copy(cckv_ref.at[layer, pg], kbuf.at[sl, pl.ds(off, page)], sem.at[0, sl]),
                pltpu.make_async_copy(ckpe_ref.at[layer, pg], rbuf.at[sl, :, pl.ds(off, page)], sem.at[1, sl]))

    def fetch(bb, sl):
        def body(p, c):
            for cp in copies(bb, sl, p):
                cp.start()
            return c
        lax.fori_loop(0, n_pages, body, 0, unroll=8)

    @pl.when(b == 0)
    def _():
        kbuf[:, past:, :] = jnp.zeros((2, LANES, KV_LORA), F32)
        fetch(0, 0)

    @pl.when(b + 1 < nb)
    def _():
        fetch(b + 1, 1 - slot)

    pltpu.make_async_copy(kbuf.at[slot, pl.ds(0, past)], kbuf.at[slot, pl.ds(0, past)], sem.at[0, slot]).wait()
    pltpu.make_async_copy(rbuf.at[slot, :, pl.ds(0, past)], rbuf.at[slot, :, pl.ds(0, past)], sem.at[1, slot]).wait()

    kbuf[slot, past:past + s_len, :] = ks_ref[...]
    rbuf[slot, :, past:] = krs_ref[0]

    rows = N_HEADS * s_len
    q = q_ref[...].reshape(rows, KV_LORA).astype(BF16)
    qr = qr_ref[...].reshape(rows, QK_ROPE).astype(BF16)
    bounds = list(range(0, past, KEY_CHUNK)) + [past]
    sizes = [min(KEY_CHUNK, past - c) for c in bounds[:-1]] + [LANES]
    ks, ss = [], []
    for c0, sz in zip(bounds, sizes):
        k = kbuf[slot, c0:c0 + sz, :].astype(BF16)
        kr_t = rbuf[slot, :, c0:c0 + sz].astype(BF16)
        s = _dot_nt(q, k) + _dot(qr, kr_t)
        if c0 == past:
            r = lax.broadcasted_iota(jnp.int32, s.shape, 0) & (s_len - 1)
            c = lax.broadcasted_iota(jnp.int32, s.shape, 1)
            s = jnp.where(c <= r, s, NEG)
        ks.append(k)
        ss.append(s)
    m = functools.reduce(jnp.maximum, [jnp.max(s, axis=-1, keepdims=True) for s in ss])
    l = jnp.zeros((rows, 1), F32)
    acc = jnp.zeros((rows, KV_LORA), F32)
    for k, s in zip(ks, ss):
        p = jnp.exp2(s - m)
        l = l + jnp.sum(p, axis=-1, keepdims=True)
        acc = acc + _dot(p.astype(BF16), k)
    o_ref[...] = (acc / l).reshape(N_HEADS, s_len, KV_LORA)


def _attn_decode(q_lat, q_pe, ckv_s, kpe_s, cache_ckv, cache_kpe, page_table, *, layer, s_len):
    db, n_pages = page_table.shape
    page = cache_ckv.shape[2]
    assert s_len & (s_len - 1) == 0 and s_len <= LANES and page % LANES == 0
    buf_rows = n_pages * page + LANES
    cache_kpe_t = jnp.swapaxes(cache_kpe, 2, 3)
    kpe_new_t = jnp.pad(jnp.swapaxes(kpe_s.reshape(db, s_len, QK_ROPE), 1, 2), ((0, 0), (0, 0), (0, LANES - s_len)))
    qmap = lambda b, pt: (0, b, 0)
    grid_spec = pltpu.PrefetchScalarGridSpec(
        num_scalar_prefetch=1,
        grid=(db,),
        in_specs=[pl.BlockSpec((N_HEADS, s_len, KV_LORA), qmap), pl.BlockSpec((N_HEADS, s_len, QK_ROPE), qmap),
                  pl.BlockSpec((s_len, KV_LORA), lambda b, pt: (b, 0)),
                  pl.BlockSpec((1, QK_ROPE, LANES), lambda b, pt: (b, 0, 0)),
                  pl.BlockSpec(memory_space=pl.ANY), pl.BlockSpec(memory_space=pl.ANY)],
        out_specs=pl.BlockSpec((N_HEADS, s_len, KV_LORA), qmap),
        scratch_shapes=[pltpu.VMEM((2, buf_rows, KV_LORA), F32), pltpu.VMEM((2, QK_ROPE, buf_rows), F32),
                        pltpu.SemaphoreType.DMA((2, 2))],
    )
    return pl.pallas_call(
        functools.partial(_attn_decode_kernel, layer=layer, n_pages=n_pages, page=page, s_len=s_len),
        grid_spec=grid_spec,
        out_shape=jax.ShapeDtypeStruct((N_HEADS, db * s_len, KV_LORA), F32),
        compiler_params=_params(("arbitrary",)),
        name="attn_decode",
    )(page_table, q_lat, q_pe, ckv_s, kpe_new_t, cache_ckv, cache_kpe_t)


def _attn_out_kernel(o_ref, x_ref, wuv_ref, wo_ref, out_ref):
    pieces = []
    for p in range(N_HEADS // 2):
        pieces.append(_dot(o_ref[2 * p].astype(BF16), wuv_ref[2 * p])
                      + _dot(o_ref[2 * p + 1].astype(BF16), wuv_ref[2 * p + 1]))
    o = jnp.concatenate(pieces, axis=-1).astype(BF16)
    out_ref[...] = x_ref[...] + _dot(o, wo_ref[...])


def _attn_out(o_lat, x, w_uv2, w_o, *, tm):
    n, d = x.shape
    assert n % tm == 0
    return pl.pallas_call(
        _attn_out_kernel,
        grid=(n // tm,),
        in_specs=[pl.BlockSpec((N_HEADS, tm, KV_LORA), lambda i: (0, i, 0)), pl.BlockSpec((tm, d), lambda i: (i, 0)),
                  _const_spec(w_uv2.shape), _const_spec(w_o.shape)],
        out_specs=pl.BlockSpec((tm, d), lambda i: (i, 0)),
        out_shape=jax.ShapeDtypeStruct((n, d), F32),
        compiler_params=_params(("parallel",)),
        name="attn_out",
    )(o_lat, x, w_uv2, w_o)


SUB = 8


def _load_rows(ref, n):
    return jnp.concatenate([ref[pl.ds(c, n, stride=SUB), :] for c in range(SUB)], axis=-1)


def _store_rows(ref, x):
    n = x.shape[0]
    for c in range(SUB):
        ref[pl.ds(c, n, stride=SUB), :] = x[:, c * LANES:(c + 1) * LANES]


def _router_kernel(x_ref, g_ref, whi_ref, wlo_ref, b_ref, h_ref, idx_ref, gate_ref):
    h = _rms(x_ref[...], g_ref[...])
    _store_rows(h_ref, h)
    h_hi = h.astype(BF16)
    h_lo = (h - h_hi.astype(F32)).astype(BF16)
    logits = _dot(h_hi, whi_ref[...]) + _dot(h_lo, whi_ref[...]) + _dot(h_hi, wlo_ref[...])
    lane = lax.broadcasted_iota(jnp.int32, logits.shape, 1)
    lane_f = lane.astype(F32)
    sel = jnp.where(lane < N_EXPERTS, logits + b_ref[...], NEG)
    m1 = jnp.max(sel, axis=-1, keepdims=True)
    i1 = jnp.min(jnp.where(sel == m1, lane_f, float(LANES)), axis=-1, keepdims=True)
    sel2 = jnp.where(lane_f == i1, NEG, sel)
    m2 = jnp.max(sel2, axis=-1, keepdims=True)
    i2 = jnp.min(jnp.where(sel2 == m2, lane_f, float(LANES)), axis=-1, keepdims=True)
    l1 = jnp.sum(jnp.where(lane_f == i1, logits, 0.0), axis=-1, keepdims=True)
    l2 = jnp.sum(jnp.where(lane_f == i2, logits, 0.0), axis=-1, keepdims=True)
    mx = jnp.maximum(l1, l2)
    e1 = jnp.exp(l1 - mx)
    e2 = jnp.exp(l2 - mx)
    den = e1 + e2
    idx_ref[...] = jnp.where(lane == 0, i1, jnp.where(lane == 1, i2, 0.0)).astype(jnp.int32)
    gate_ref[...] = jnp.where(lane == 0, e1 / den, jnp.where(lane == 1, e2 / den, 0.0))


def _router(x, g, w_hi, w_lo, bias, *, tm):
    n, d = x.shape
    assert n % tm == 0 and d == SUB * LANES
    tok = lambda width: pl.BlockSpec((tm, width), lambda i: (i, 0))
    return pl.pallas_call(
        _router_kernel,
        grid=(n // tm,),
        in_specs=[tok(d), _const_spec((1, d)), _const_spec(w_hi.shape), _const_spec(w_lo.shape),
                  _const_spec((1, LANES))],
        out_specs=[pl.BlockSpec((tm * SUB, LANES), lambda i: (i, 0)), tok(LANES), tok(LANES)],
        out_shape=[jax.ShapeDtypeStruct((n * SUB, LANES), F32), jax.ShapeDtypeStruct((n, LANES), jnp.int32),
                   jax.ShapeDtypeStruct((n, LANES), F32)],
        compiler_params=_params(("parallel",)),
        name="moe_router",
    )(x, g.reshape(1, d), w_hi, w_lo, bias)


def _dispatch_kernel(pos_ref, h_ref, init_ref, out_ref, sem, *, tm):
    del init_ref
    base = pl.program_id(0) * tm

    def start(r, c):
        src = h_ref.at[pl.ds(pl.multiple_of(r * SUB, SUB), SUB)]
        for k in range(TOP_K):
            dst = pl.multiple_of(pos_ref[TOP_K * (base + r) + k] * SUB, SUB)
            pltpu.make_async_copy(src, out_ref.at[pl.ds(dst, SUB)], sem).start()
        return c

    lax.fori_loop(0, tm, start, 0, unroll=8)
    for k in range(TOP_K):
        pltpu.make_async_copy(h_ref, out_ref.at[pl.ds(0, tm * SUB)], sem).wait()


def _dispatch(pos, h, n_rows, *, tm):
    n = h.shape[0] // SUB
    assert n % tm == 0
    grid_spec = pltpu.PrefetchScalarGridSpec(
        num_scalar_prefetch=1,
        grid=(n // tm,),
        in_specs=[pl.BlockSpec((tm * SUB, LANES), lambda i, pos: (i, 0)), pl.BlockSpec(memory_space=pl.ANY)],
        out_specs=pl.BlockSpec(memory_space=pl.ANY),
        scratch_shapes=[pltpu.SemaphoreType.DMA(())],
    )
    return pl.pallas_call(
        functools.partial(_dispatch_kernel, tm=tm),
        grid_spec=grid_spec,
        out_shape=jax.ShapeDtypeStruct((n_rows * SUB, LANES), F32),
        input_output_aliases={2: 0},
        compiler_params=_params(("arbitrary",)),
        name="moe_dispatch",
    )(pos, h, jnp.zeros((n_rows * SUB, LANES), F32))


def _expert_kernel(te_ref, x_ref, wg_ref, wu_ref, wd_ref, o_ref, acc_ref, *, n_tiles):
    t = pl.program_id(0)

    tm = acc_ref.shape[0]

    @pl.when(t < te_ref[n_tiles])
    def _():
        _swiglu_into(acc_ref, _load_rows(x_ref, tm).astype(BF16), wg_ref, wu_ref, wd_ref, lead=(0,))
        _store_rows(o_ref, acc_ref[...])

    @pl.when(t >= te_ref[n_tiles])
    def _():
        o_ref[...] = jnp.zeros(o_ref.shape, F32)


def _expert_ffn(te, xs, wg, wu, wd, *, tm):
    r = xs.shape[0] // SUB
    n_tiles = r // tm
    wmap = lambda t, te: (te[t], 0, 0)
    grid_spec = pltpu.PrefetchScalarGridSpec(
        num_scalar_prefetch=1,
        grid=(n_tiles,),
        in_specs=[pl.BlockSpec((tm * SUB, LANES), lambda t, te: (t, 0)),
                  pl.BlockSpec((1,) + wg.shape[1:], wmap), pl.BlockSpec((1,) + wu.shape[1:], wmap),
                  pl.BlockSpec((1,) + wd.shape[1:], wmap)],
        out_specs=pl.BlockSpec((tm * SUB, LANES), lambda t, te: (t, 0)),
        scratch_shapes=[pltpu.VMEM((tm, SUB * LANES), F32)],
    )
    return pl.pallas_call(
        functools.partial(_expert_kernel, n_tiles=n_tiles),
        grid_spec=grid_spec,
        out_shape=jax.ShapeDtypeStruct(xs.shape, F32),
        compiler_params=_params(("arbitrary",)),
        name="moe_experts",
    )(te, xs, wg, wu, wd)


def _combine_kernel(pos_ref, x_ref, gate_ref, gf_ref, y_ref, o_ref, buf, sem, *, tm, final_norm):
    base = pl.program_id(0) * tm

    def start(r, c):
        row = pl.multiple_of(r * SUB, SUB)
        for k in range(TOP_K):
            src = pl.multiple_of(pos_ref[TOP_K * (base + r) + k] * SUB, SUB)
            pltpu.make_async_copy(y_ref.at[pl.ds(src, SUB)], buf.at[k, pl.ds(row, SUB)], sem).start()
        return c

    lax.fori_loop(0, tm, start, 0, unroll=8)
    for k in range(TOP_K):
        pltpu.make_async_copy(y_ref.at[pl.ds(0, tm * SUB)], buf.at[k], sem).wait()
    gate = gate_ref[...]
    out = x_ref[...] + gate[:, 0:1] * _load_rows(buf.at[0], tm) + gate[:, 1:2] * _load_rows(buf.at[1], tm)
    if final_norm:
        out = _rms(out, gf_ref[...])
    o_ref[...] = out


def _combine(pos, x, gates, g_final, y, *, tm, final_norm):
    n, d = x.shape
    assert n % tm == 0
    grid_spec = pltpu.PrefetchScalarGridSpec(
        num_scalar_prefetch=1,
        grid=(n // tm,),
        in_specs=[pl.BlockSpec((tm, d), lambda i, pos: (i, 0)), pl.BlockSpec((tm, LANES), lambda i, pos: (i, 0)),
                  pl.BlockSpec((1, d), lambda i, pos: (0, 0)), pl.BlockSpec(memory_space=pl.ANY)],
        out_specs=pl.BlockSpec((tm, d), lambda i, pos: (i, 0)),
        scratch_shapes=[pltpu.VMEM((TOP_K, tm * SUB, LANES), F32), pltpu.SemaphoreType.DMA(())],
    )
    return pl.pallas_call(
        functools.partial(_combine_kernel, tm=tm, final_norm=final_norm),
        grid_spec=grid_spec,
        out_shape=jax.ShapeDtypeStruct((n, d), F32),
        compiler_params=_params(("arbitrary",)),
        name="moe_combine",
    )(pos, x, gates, g_final.reshape(1, d), y)


def _route_positions(idx, tm, n_tiles):
    e = idx[:, :TOP_K].reshape(-1)
    onehot = (e[:, None] == jnp.arange(N_EXPERTS, dtype=jnp.int32)[None, :]).astype(jnp.int32)
    csum = jnp.cumsum(onehot, axis=0)
    rank = jnp.sum(csum * onehot, axis=1) - 1
    counts = csum[-1]
    padded = (counts + tm - 1) // tm * tm
    ends = jnp.cumsum(padded)
    pos = (ends - padded)[e] + rank
    n_used = ends[-1] // tm
    starts = jnp.arange(n_tiles, dtype=jnp.int32) * tm
    te = jnp.sum((starts[:, None] >= ends[None, :]).astype(jnp.int32), axis=1)
    te = jnp.minimum(te, jnp.take(te, jnp.maximum(n_used - 1, 0)))
    return pos.astype(jnp.int32), jnp.concatenate([te, n_used[None]]).astype(jnp.int32)


def _moe(x, g, w, g_final, *, tm, tile, final_norm):
    n, _ = x.shape
    h, idx, gates = _router(x, g, w["r_hi"], w["r_lo"], w["r_b"], tm=tm)
    n_tiles = -(-TOP_K * n // tile) + N_EXPERTS
    pos, te = _route_positions(idx, tile, n_tiles)
    xs = _dispatch(pos, h, n_tiles * tile, tm=tm)
    ys = _expert_ffn(te, xs, w["wg"], w["wu"], w["wd"], tm=tile)
    return _combine(pos, x, gates, g_final, ys, tm=tm, final_norm=final_norm)


def _rope_tables(pos, rows):
    half = QK_ROPE // 2
    freqs = 1.0 / (ROPE_BASE ** (jnp.arange(half, dtype=F32) * (2.0 / QK_ROPE)))
    ang = pos.astype(F32)[:, None] * freqs[None, :]
    cos, sin = jnp.cos(ang), jnp.sin(ang)
    cos = jnp.tile(jnp.concatenate([cos, cos], axis=-1), (rows // pos.shape[0], N_HEADS))
    sin = jnp.tile(jnp.concatenate([-sin, sin], axis=-1), (rows // pos.shape[0], N_HEADS))
    return cos, sin


def _mla_weights(w_in, g_q, g_kv, w_uq, w_uk, w_uv, w_o):
    half = QK_ROPE // 2
    swap = jnp.concatenate([jnp.arange(half, QK_ROPE), jnp.arange(half)])
    c2 = Q_LORA + KV_LORA
    w_in_ext = jnp.concatenate([w_in, w_in[:, c2:][:, swap]], axis=1)
    uq = w_uq.reshape(Q_LORA, N_HEADS, QK_NOPE + QK_ROPE)
    rope_cols = uq[:, :, QK_NOPE:]
    a = jnp.transpose(w_uk, (1, 2, 0)).reshape(N_HEADS // 4, 4, QK_NOPE, KV_LORA)
    w_uk4 = jnp.einsum("pjnc,jk->pjnkc", a, jnp.eye(4, dtype=a.dtype)).reshape(N_HEADS // 4, 4 * QK_NOPE, 4 * KV_LORA)
    side = jax.nn.one_hot(jnp.arange(N_HEADS) % 2, 2, dtype=w_uv.dtype)
    w_uv2 = jnp.einsum("chv,hk->hckv", w_uv, side).reshape(N_HEADS, KV_LORA, 2 * V_HEAD)
    return {
        "w_in": w_in_ext.astype(BF16), "g_q": g_q.reshape(1, -1), "g_kv": g_kv.reshape(1, -1),
        "w_qn": uq[:, :, :QK_NOPE].reshape(Q_LORA, -1).astype(BF16),
        "w_qa": rope_cols.reshape(Q_LORA, -1).astype(BF16),
        "w_qb": rope_cols[:, :, swap].reshape(Q_LORA, -1).astype(BF16),
        "w_uk4": w_uk4.astype(BF16), "w_uv2": w_uv2.astype(BF16), "w_o": w_o.astype(BF16),
    }


def _moe_weights(w_router, b_router, w_gate, w_up, w_down):
    d = w_router.shape[0]
    wr = jnp.zeros((d, LANES), F32).at[:, :N_EXPERTS].set(w_router)
    hi = wr.astype(BF16)
    return {
        "r_hi": hi, "r_lo": (wr - hi.astype(F32)).astype(BF16),
        "r_b": jnp.zeros((1, LANES), F32).at[0, :N_EXPERTS].set(b_router),
        "wg": w_gate.astype(BF16), "wu": w_up.astype(BF16), "wd": w_down.astype(BF16),
    }


def _tile(n, want):
    if n <= want:
        return n
    t = want - want % 16
    while n % t:
        t -= 16
    return t


def kernel(x_prompt, x_sample, cache_ckv, cache_kpe, state_pool, page_table, meta_tokens, g_mix, g_ffn, g_final, pool_w, pool_scale, mla_w_in, mla_g_q, mla_g_kv, mla_w_uq, mla_w_uk, mla_w_uv, mla_w_o, ffn_w_gate, ffn_w_up, ffn_w_down, moe_w_router, moe_b_router, moe_w_gate, moe_w_up, moe_w_down):
    b, seq, d = x_prompt.shape
    db, s_len, _ = x_sample.shape
    depth = g_mix.shape[0]
    assert depth % 2 == 0
    past_len = page_table.shape[1] * cache_ckv.shape[2]
    n_main, n_samp = b * seq, db * s_len

    xm = x_prompt
    xs = x_sample
    xe = meta_tokens[None].astype(F32)

    tm_main = _tile(n_main, 1024)
    tm_samp = _tile(n_samp, 512)
    cos_m, sin_m = _rope_tables(N_META + jnp.arange(seq), max(seq, tm_main))
    cos_s, sin_s = _rope_tables(past_len + jnp.arange(s_len), tm_samp)
    cos_e, sin_e = _rope_tables(jnp.arange(N_META), N_META)

    outs = {k: [] for k in ("ckv_p", "kpe_p", "pool_p", "ckv_s", "kpe_s", "pool_s")}
    for i in range(depth):
        j = i // 2
        last = i == depth - 1
        if i % 2 == 0:
            w_bf = pool_w[j].astype(BF16)
            zeros_e = jnp.zeros((1, HIST, d), F32)
            xe, tail_e = _pool_layer(xe, zeros_e, g_mix[i], w_bf, pool_scale[j], has_hist=False, tb=1, tl=N_META)
            xm, tail_m = _pool_layer(xm, tail_e, g_mix[i], w_bf, pool_scale[j], has_hist=True, tb=1,
                                     tl=min(512, seq))
            prev_s = jnp.concatenate([jnp.zeros((db, 1, d), F32), state_pool[j]], axis=1)
            xs, tail_s = _pool_layer(xs, prev_s, g_mix[i], w_bf, pool_scale[j], has_hist=True,
                                     tb=_tile(db, 64), tl=s_len)
            outs["pool_p"].append(tail_m[:, 1:])
            outs["pool_s"].append(tail_s[:, 1:])

            wg, wu, wd = ffn_w_gate[j].astype(BF16), ffn_w_up[j].astype(BF16), ffn_w_down[j].astype(BF16)
            xm = _ffn_dense(xm.reshape(n_main, d), g_ffn[i], wg, wu, wd, tm=tm_main).reshape(b, seq, d)
            small = jnp.concatenate([xs.reshape(n_samp, d), xe[0]], axis=0)
            small = _ffn_dense(small, g_ffn[i], wg, wu, wd, tm=small.shape[0])
            xs, xe = small[:n_samp].reshape(db, s_len, d), small[n_samp:][None]
        else:
            w = _mla_weights(mla_w_in[j], mla_g_q[j], mla_g_kv[j], mla_w_uq[j], mla_w_uk[j], mla_w_uv[j], mla_w_o[j])
            xe2, xm2, xs2 = xe[0], xm.reshape(n_main, d), xs.reshape(n_samp, d)
            ql_e, qp_e, ckv_e, kpe_e, ckvb_e, kpeb_e = _mla_proj(xe2, cos_e, sin_e, g_mix[i], w, tm=N_META, q_dtype=BF16)
            ql_m, qp_m, ckv_m, kpe_m, ckvb_m, kpeb_m = _mla_proj(xm2, cos_m, sin_m, g_mix[i], w, tm=tm_main, q_dtype=BF16)
            ql_s, qp_s, ckv_s, kpe_s, _, _ = _mla_proj(xs2, cos_s, sin_s, g_mix[i], w, tm=tm_samp, q_dtype=F32)

            o_e = _attn_causal(ql_e, qp_e, ckvb_e, kpeb_e, None, batch=1, tqb=N_META, tqs=N_META, tk=LANES)
            o_m = _attn_causal(ql_m, qp_m, ckvb_m, kpeb_m, (ckvb_e, kpeb_e), batch=b, tqb=min(256, seq), tqs=64,
                               tk=512)
            o_s = _attn_decode(ql_s, qp_s, ckv_s, kpe_s, cache_ckv, cache_kpe, page_table, layer=j, s_len=s_len)

            xe2 = _attn_out(o_e, xe2, w["w_uv2"], w["w_o"], tm=N_META)
            xm2 = _attn_out(o_m, xm2, w["w_uv2"], w["w_o"], tm=tm_main)
            xs2 = _attn_out(o_s, xs2, w["w_uv2"], w["w_o"], tm=tm_samp)

            outs["ckv_p"].append(jnp.concatenate(
                [jnp.broadcast_to(ckv_e[None], (b, N_META, KV_LORA)), ckv_m.reshape(b, seq, KV_LORA)], axis=1))
            outs["kpe_p"].append(jnp.concatenate(
                [jnp.broadcast_to(kpe_e[None], (b, N_META, QK_ROPE)), kpe_m.reshape(b, seq, QK_ROPE)], axis=1))
            outs["ckv_s"].append(ckv_s.reshape(db, s_len, KV_LORA))
            outs["kpe_s"].append(kpe_s.reshape(db, s_len, QK_ROPE))

            mw = _moe_weights(moe_w_router[j], moe_b_router[j], moe_w_gate[j], moe_w_up[j], moe_w_down[j])
            xm = _moe(xm2, g_ffn[i], mw, g_final, tm=_tile(n_main, 512), tile=512, final_norm=last).reshape(b, seq, d)
            small = jnp.concatenate([xs2, xe2], axis=0)
            small = _moe(small, g_ffn[i], mw, g_final, tm=small.shape[0], tile=128, final_norm=last)
            xs, xe = small[:n_samp].reshape(db, s_len, d), small[n_samp:][None]

    return (xm, xs, jnp.stack(outs["ckv_p"]), jnp.stack(outs["kpe_p"]), jnp.stack(outs["pool_p"]),
            jnp.stack(outs["ckv_s"]), jnp.stack(outs["kpe_s"]), jnp.stack(outs["pool_s"]))
```

```python
import functools

import jax
import jax.numpy as jnp
from jax import lax
from jax.experimental import pallas as pl
from jax.experimental.pallas import tpu as pltpu

EPS = 1e-6
N_META = 16
POOL_WINDOWS = (2, 4, 8, 16)
N_HEADS = 16
Q_LORA = 384
KV_LORA = 256
QK_NOPE = 64
QK_ROPE = 32
V_HEAD = 64
ROPE_BASE = 10000.0
ATTN_SCALE = (QK_NOPE + QK_ROPE) ** -0.5
Q_SCALE = ATTN_SCALE * 1.4426950408889634
N_EXPERTS = 8
TOP_K = 2

HIST = 16
LANES = 128
NEG = -1e30
VMEM_LIMIT = 56 * 1024 * 1024
BF16 = jnp.bfloat16
F32 = jnp.float32


def _dot(a, b):
    return jnp.dot(a, b, preferred_element_type=F32)


def _dot_nt(a, b):
    return lax.dot_general(a, b, (((1,), (1,)), ((), ())), preferred_element_type=F32)


def _rms(x, g):
    return x * lax.rsqrt(jnp.mean(x * x, axis=-1, keepdims=True) + EPS) * g


def _params(sem, vmem=VMEM_LIMIT):
    return pltpu.CompilerParams(dimension_semantics=sem, vmem_limit_bytes=vmem)


def _const_spec(shape):
    nd = len(shape)
    return pl.BlockSpec(shape, lambda *_: (0,) * nd)


def _pool_kernel(*refs, has_hist, multi_tile, tb, tl):
    if multi_tile:
        x_ref, halo_ref, prev_ref, g_ref, w_ref, sc_ref, o_ref, tail_ref, hext_ref = refs
    else:
        x_ref, prev_ref, g_ref, w_ref, sc_ref, o_ref, tail_ref, hext_ref = refs
        halo_ref = None
    t = pl.program_id(1)
    d = x_ref.shape[-1]
    gc = d // len(POOL_WINDOWS)
    g = g_ref[...]
    x = x_ref[...]
    h = _rms(x, g)
    hext_ref[:, HIST:, :] = h
    prev = jnp.broadcast_to(prev_ref[...], (tb, HIST, d))
    if multi_tile:
        @pl.when(t == 0)
        def _():
            hext_ref[:, :HIST, :] = prev

        @pl.when(t > 0)
        def _():
            hext_ref[:, :HIST, :] = _rms(halo_ref[...], g)
    else:
        hext_ref[:, :HIST, :] = prev

    for gi, w in enumerate(POOL_WINDOWS):
        c0, c1 = gi * gc, (gi + 1) * gc
        win = hext_ref[:, pl.ds(HIST, tl), c0:c1]
        for i in range(1, w):
            win = win + hext_ref[:, pl.ds(HIST - i, tl), c0:c1]
        hg = hext_ref[:, pl.ds(HIST, tl), c0:c1]
        if has_hist:
            dg = win * (1.0 / w) - hg
        else:
            row = lax.broadcasted_iota(jnp.int32, (tb, tl, gc), 1) + t * tl
            cnt = jnp.minimum(row + 1, w).astype(F32)
            dg = win / cnt - hg
        dg = dg.reshape(tb * tl, gc).astype(BF16)
        yg = _dot(dg, w_ref[gi]).reshape(tb, tl, gc)
        o_ref[:, :, c0:c1] = x_ref[:, :, c0:c1] + yg * sc_ref[:, c0:c1]
    tail_ref[...] = hext_ref[:, pl.ds(tl, HIST), :]


def _pool_layer(x, prev, g, w_bf, scale, *, has_hist, tb, tl):
    b, l, d = x.shape
    n_t = l // tl
    multi = n_t > 1
    assert l % tl == 0 and b % tb == 0 and (not multi or (tb == 1 and tl % HIST == 0))
    shared = prev.shape[0] == 1
    per_tile = tl // HIST
    in_specs = [pl.BlockSpec((tb, tl, d), lambda i, t: (i, t, 0))]
    args = [x]
    if multi:
        in_specs.append(pl.BlockSpec((tb, HIST, d), lambda i, t: (i, jnp.maximum(t * per_tile - 1, 0), 0)))
        args.append(x)
    in_specs += [
        pl.BlockSpec((1 if shared else tb, HIST, d), (lambda i, t: (0, 0, 0)) if shared else (lambda i, t: (i, 0, 0))),
        _const_spec((1, d)), _const_spec(w_bf.shape), _const_spec((1, d)),
    ]
    args += [prev, g.reshape(1, d), w_bf, scale.reshape(1, d)]
    return pl.pallas_call(
        functools.partial(_pool_kernel, has_hist=has_hist, multi_tile=multi, tb=tb, tl=tl),
        grid=(b // tb, n_t),
        in_specs=in_specs,
        out_specs=[pl.BlockSpec((tb, tl, d), lambda i, t: (i, t, 0)),
                   pl.BlockSpec((tb, HIST, d), lambda i, t: (i, 0, 0))],
        out_shape=[jax.ShapeDtypeStruct((b, l, d), F32), jax.ShapeDtypeStruct((b, HIST, d), F32)],
        scratch_shapes=[pltpu.VMEM((tb, tl + HIST, d), F32)],
        compiler_params=_params(("parallel", "arbitrary")),
        name="pool_mixer",
    )(*args)


FF_CHUNK = 256


def _swiglu_into(acc_ref, h, wg_ref, wu_ref, wd_ref, lead=()):
    n_ff = wg_ref.shape[-1]
    for c in range(0, n_ff, FF_CHUNK):
        gg = _dot(h, wg_ref[lead + (slice(None), slice(c, c + FF_CHUNK))])
        uu = _dot(h, wu_ref[lead + (slice(None), slice(c, c + FF_CHUNK))])
        a = (gg * jax.nn.sigmoid(gg) * uu).astype(BF16)
        y = _dot(a, wd_ref[lead + (slice(c, c + FF_CHUNK), slice(None))])
        if c == 0:
            acc_ref[...] = y
        else:
            acc_ref[...] += y


def _ffn_kernel(x_ref, g_ref, wg_ref, wu_ref, wd_ref, o_ref, acc_ref):
    x = x_ref[...]
    h = _rms(x, g_ref[...]).astype(BF16)
    _swiglu_into(acc_ref, h, wg_ref, wu_ref, wd_ref)
    o_ref[...] = x + acc_ref[...]


def _ffn_dense(x, g, wg, wu, wd, *, tm):
    n, d = x.shape
    assert n % tm == 0
    resident = lambda shape: pl.BlockSpec(shape, lambda i: (0, 0), pipeline_mode=pl.Buffered(1))
    return pl.pallas_call(
        _ffn_kernel,
        grid=(n // tm,),
        in_specs=[pl.BlockSpec((tm, d), lambda i: (i, 0)), _const_spec((1, d)),
                  resident(wg.shape), resident(wu.shape), resident(wd.shape)],
        out_specs=pl.BlockSpec((tm, d), lambda i: (i, 0)),
        out_shape=jax.ShapeDtypeStruct((n, d), F32),
        scratch_shapes=[pltpu.VMEM((tm, d), F32)],
        compiler_params=_params(("parallel",)),
        name="ffn_dense",
    )(x, g.reshape(1, d), wg, wu, wd)


def _mla_proj_kernel(x_ref, cos_ref, sin_ref, g_ref, win_ref, gq_ref, gkv_ref, wqn_ref, wqa_ref, wqb_ref, wuk_ref,
                     qlat_ref, qpe_ref, ckv_ref, kpe_ref, ckvb_ref, kpeb_ref):
    h = _rms(x_ref[...], g_ref[...]).astype(BF16)
    proj = _dot(h, win_ref[...])
    c1, c2, c3 = Q_LORA, Q_LORA + KV_LORA, Q_LORA + KV_LORA + QK_ROPE
    cqn = _rms(proj[:, :c1], gq_ref[...]).astype(BF16)
    ckv = _rms(proj[:, c1:c2], gkv_ref[...])
    cos = cos_ref[...]
    sin = sin_ref[...]
    kpe = proj[:, c2:c3] * cos[:, :QK_ROPE] + proj[:, c3:c3 + QK_ROPE] * sin[:, :QK_ROPE]
    ckv_ref[...] = ckv
    kpe_ref[...] = kpe
    ckvb_ref[...] = ckv.astype(BF16)
    kpeb_ref[...] = kpe.astype(BF16)

    qn = _dot(cqn, wqn_ref[...]).astype(BF16)
    quad = 4 * QK_NOPE
    for p in range(N_HEADS // 4):
        ql = _dot(qn[:, p * quad:(p + 1) * quad], wuk_ref[p]) * Q_SCALE
        for j in range(4):
            qlat_ref[4 * p + j] = ql[:, j * KV_LORA:(j + 1) * KV_LORA].astype(qlat_ref.dtype)
    qr = (_dot(cqn, wqa_ref[...]) * cos + _dot(cqn, wqb_ref[...]) * sin) * Q_SCALE
    for hd in range(N_HEADS):
        qpe_ref[hd] = qr[:, hd * QK_ROPE:(hd + 1) * QK_ROPE].astype(qpe_ref.dtype)


def _mla_proj(x, cos, sin, g, w, *, tm, q_dtype):
    n, d = x.shape
    assert n % tm == 0 and cos.shape[0] % tm == 0
    n_tab = cos.shape[0] // tm
    hr = N_HEADS * QK_ROPE
    tok = lambda width: pl.BlockSpec((tm, width), lambda i: (i, 0))
    tab = pl.BlockSpec((tm, hr), lambda i: (i % n_tab, 0))
    headed = lambda width: pl.BlockSpec((N_HEADS, tm, width), lambda i: (0, i, 0))
    return pl.pallas_call(
        _mla_proj_kernel,
        grid=(n // tm,),
        in_specs=[tok(d), tab, tab, _const_spec((1, d)), _const_spec(w["w_in"].shape),
                  _const_spec((1, Q_LORA)), _const_spec((1, KV_LORA)), _const_spec(w["w_qn"].shape),
                  _const_spec(w["w_qa"].shape), _const_spec(w["w_qb"].shape), _const_spec(w["w_uk4"].shape)],
        out_specs=[headed(KV_LORA), headed(QK_ROPE), tok(KV_LORA), tok(QK_ROPE), tok(KV_LORA), tok(QK_ROPE)],
        out_shape=[jax.ShapeDtypeStruct((N_HEADS, n, KV_LORA), q_dtype),
                   jax.ShapeDtypeStruct((N_HEADS, n, QK_ROPE), q_dtype),
                   jax.ShapeDtypeStruct((n, KV_LORA), F32), jax.ShapeDtypeStruct((n, QK_ROPE), F32),
                   jax.ShapeDtypeStruct((n, KV_LORA), BF16), jax.ShapeDtypeStruct((n, QK_ROPE), BF16)],
        compiler_params=_params(("parallel",)),
        name="mla_proj",
    )(x, cos, sin, g.reshape(1, d), w["w_in"], w["g_q"], w["g_kv"], w["w_qn"], w["w_qa"], w["w_qb"], w["w_uk4"])


ATTN_SPLIT = 2
ATTN_EDGE = 256
ROW_CHUNK = 32


def _attn_kernel(*refs, n_pre, t_len, tqb, tqs, tk):
    if n_pre:
        q_ref, qr_ref, k_ref, kr_ref, pk_ref, pkr_ref, o_ref, kf, krf, m_ref, l_ref, a_ref, acc_ref, s_ref, p_ref = refs
    else:
        q_ref, qr_ref, k_ref, kr_ref, o_ref, kf, krf, m_ref, l_ref, a_ref, acc_ref, s_ref, p_ref = refs
    qi = pl.program_id(1)
    n_keys = n_pre + t_len
    hg = N_HEADS // ATTN_SPLIT
    rows = hg * tqs
    tk_shift = tk.bit_length() - 1
    te = min(tk, ATTN_EDGE)
    te_shift = te.bit_length() - 1

    @pl.when(qi == 0)
    def _():
        if n_pre:
            kf[0:n_pre, :] = pk_ref[...]
            krf[0:n_pre, :] = pkr_ref[...]
        kf[n_pre:n_keys, :] = k_ref[...]
        krf[n_pre:n_keys, :] = kr_ref[...]
        if kf.shape[0] > n_keys:
            kf[n_keys:, :] = jnp.zeros((kf.shape[0] - n_keys, KV_LORA), kf.dtype)
            krf[n_keys:, :] = jnp.zeros((kf.shape[0] - n_keys, QK_ROPE), krf.dtype)

    def sub_block(i, carry):
        t0 = pl.multiple_of(i * tqs, tqs)
        p0 = n_pre + qi * tqb + i * tqs
        for g in range(ATTN_SPLIT):
            m_ref[g] = jnp.full((rows, LANES), NEG, F32)
            l_ref[g] = jnp.zeros((rows, LANES), F32)
            acc_ref[g] = jnp.zeros((rows, KV_LORA), F32)

        def update(k0, width, masked):
            k = kf[pl.ds(k0, width), :]
            kr = krf[pl.ds(k0, width), :]
            n_rep = width // LANES
            chunks = [slice(c, c + ROW_CHUNK) for c in range(0, rows, ROW_CHUNK)]
            for g in range(ATTN_SPLIT):
                q = q_ref[g * hg:(g + 1) * hg, pl.ds(t0, tqs), :].reshape(rows, KV_LORA)
                qr = qr_ref[g * hg:(g + 1) * hg, pl.ds(t0, tqs), :].reshape(rows, QK_ROPE)
                s = _dot_nt(q, k) + _dot_nt(qr, kr)
                if masked:
                    r = lax.broadcasted_iota(jnp.int32, s.shape, 0) & (tqs - 1)
                    c = lax.broadcasted_iota(jnp.int32, s.shape, 1)
                    s = jnp.where(c - r <= p0 - k0, s, NEG)
                s_ref[g, :, :width] = s
            for g in range(ATTN_SPLIT):
                for rs in chunks:
                    m_old = m_ref[g, rs, :]
                    row_max = jnp.max(s_ref[g, rs, :width], axis=-1, keepdims=True)
                    m_new = jnp.maximum(m_old, jnp.broadcast_to(row_max, (ROW_CHUNK, LANES)))
                    a_ref[g, rs, :] = jnp.exp2(m_old - m_new)
                    m_ref[g, rs, :] = m_new
                for rs in chunks:
                    p = jnp.exp2(s_ref[g, rs, :width] - jnp.concatenate([m_ref[g, rs, :]] * n_rep, axis=-1))
                    part = p[:, :LANES]
                    for c in range(1, n_rep):
                        part = part + p[:, c * LANES:(c + 1) * LANES]
                    l_ref[g, rs, :] = a_ref[g, rs, :] * l_ref[g, rs, :] + part
                    p_ref[g, rs, :width] = p.astype(BF16)
            for g in range(ATTN_SPLIT):
                alpha = jnp.concatenate([a_ref[g]] * (KV_LORA // LANES), axis=-1)
                acc_ref[g] = alpha * acc_ref[g] + _dot(p_ref[g, :, :width], k)

        def full_tile(j, c):
            update(pl.multiple_of(j * tk, tk), tk, False)
            return c

        def edge_tile(j, c):
            update(pl.multiple_of(j * te, te), te, True)
            return c

        n_full = (p0 + 1) >> tk_shift
        lax.fori_loop(0, n_full, full_tile, 0)
        lax.fori_loop(n_full * (tk // te), ((p0 + tqs - 1) >> te_shift) + 1, edge_tile, 0)
        for g in range(ATTN_SPLIT):
            o = acc_ref[g] / jnp.sum(l_ref[g], axis=-1, keepdims=True)
            o_ref[g * hg:(g + 1) * hg, pl.ds(t0, tqs), :] = o.reshape(hg, tqs, KV_LORA).astype(o_ref.dtype)
        return carry

    lax.fori_loop(0, tqb // tqs, sub_block, 0)


def _attn_causal(q_lat, q_pe, ckv_b, kpe_b, prefix, *, batch, tqb, tqs, tk):
    n = ckv_b.shape[0]
    t_len = n // batch
    n_pre = 0 if prefix is None else prefix[0].shape[0]
    assert t_len % tqb == 0 and tqb % tqs == 0 and tk % tqs == 0 and tk & (tk - 1) == 0 and tqs & (tqs - 1) == 0
    assert tqs % 16 == 0 and n_pre % 16 == 0 and (n_pre + t_len) % 16 == 0
    n_q = t_len // tqb
    key_rows = -(-(n_pre + t_len) // tk) * tk
    qmap = lambda b, qi: (0, b * n_q + qi, 0)
    kmap = lambda b, qi: (b, 0)
    in_specs = [pl.BlockSpec((N_HEADS, tqb, KV_LORA), qmap), pl.BlockSpec((N_HEADS, tqb, QK_ROPE), qmap),
                pl.BlockSpec((t_len, KV_LORA), kmap), pl.BlockSpec((t_len, QK_ROPE), kmap)]
    args = [q_lat, q_pe, ckv_b, kpe_b]
    if n_pre:
        in_specs += [_const_spec(prefix[0].shape), _const_spec(prefix[1].shape)]
        args += list(prefix)
    rows = N_HEADS // ATTN_SPLIT * tqs
    return pl.pallas_call(
        functools.partial(_attn_kernel, n_pre=n_pre, t_len=t_len, tqb=tqb, tqs=tqs, tk=tk),
        grid=(batch, n_q),
        in_specs=in_specs,
        out_specs=pl.BlockSpec((N_HEADS, tqb, KV_LORA), qmap),
        out_shape=jax.ShapeDtypeStruct((N_HEADS, n, KV_LORA), q_lat.dtype),
        scratch_shapes=[pltpu.VMEM((key_rows, KV_LORA), BF16), pltpu.VMEM((key_rows, QK_ROPE), BF16),
                        pltpu.VMEM((ATTN_SPLIT, rows, LANES), F32), pltpu.VMEM((ATTN_SPLIT, rows, LANES), F32),
                        pltpu.VMEM((ATTN_SPLIT, rows, LANES), F32), pltpu.VMEM((ATTN_SPLIT, rows, KV_LORA), F32),
                        pltpu.VMEM((ATTN_SPLIT, rows, tk), F32), pltpu.VMEM((ATTN_SPLIT, rows, tk), BF16)],
        compiler_params=_params(("parallel", "arbitrary")),
        name="attn_causal",
    )(*args)


KEY_CHUNK = 2048


def _attn_decode_kernel(pt_ref, q_ref, qr_ref, ks_ref, krs_ref, cckv_ref, ckpe_ref, o_ref,
                        kbuf, rbuf, sem, *, layer, n_pages, page, s_len):
    b = pl.program_id(0)
    nb = pl.num_programs(0)
    slot = b % 2
    past = n_pages * page

    def copies(bb, sl, p):
        pg = pt_ref[bb, p]
        off = pl.multiple_of(p * page, page)
        return (pltpu.make_async_copy(cckv_ref.at[layer, pg], kbuf.at[sl, pl.ds(off, page)], sem.at[0, sl]),
                pltpu.make_async_copy(ckpe_ref.at[layer, pg], rbuf.at[sl, :, pl.ds(off, page)], sem.at[1, sl]))

    def fetch(bb, sl):
        def body(p, c):
            for cp in copies(bb, sl, p):
                cp.start()
            return c
        lax.fori_loop(0, n_pages, body, 0, unroll=8)

    @pl.when(b == 0)
    def _():
        kbuf[:, past:, :] = jnp.zeros((2, LANES, KV_LORA), F32)
        fetch(0, 0)

    @pl.when(b + 1 < nb)
    def _():
        fetch(b + 1, 1 - slot)

    pltpu.make_async_copy(kbuf.at[slot, pl.ds(0, past)], kbuf.at[slot, pl.ds(0, past)], sem.at[0, slot]).wait()
    pltpu.make_async_copy(rbuf.at[slot, :, pl.ds(0, past)], rbuf.at[slot, :, pl.ds(0, past)], sem.at[1, slot]).wait()

    kbuf[slot, past:past + s_len, :] = ks_ref[...]
    rbuf[slot, :, past:] = krs_ref[0]

    rows = N_HEADS * s_len
    q = q_ref[...].reshape(rows, KV_LORA).astype(BF16)
    qr = qr_ref[...].reshape(rows, QK_ROPE).astype(BF16)
    bounds = list(range(0, past, KEY_CHUNK)) + [past]
    sizes = [min(KEY_CHUNK, past - c) for c in bounds[:-1]] + [LANES]
    ks, ss = [], []
    for c0, sz in zip(bounds, sizes):
        k = kbuf[slot, c0:c0 + sz, :].astype(BF16)
        kr_t = rbuf[slot, :, c0:c0 + sz].astype(BF16)
        s = _dot_nt(q, k) + _dot(qr, kr_t)
        if c0 == past:
            r = lax.broadcasted_iota(jnp.int32, s.shape, 0) & (s_len - 1)
            c = lax.broadcasted_iota(jnp.int32, s.shape, 1)
            s = jnp.where(c <= r, s, NEG)
        ks.append(k)
        ss.append(s)
    m = functools.reduce(jnp.maximum, [jnp.max(s, axis=-1, keepdims=True) for s in ss])
    l = jnp.zeros((rows, 1), F32)
    acc = jnp.zeros((rows, KV_LORA), F32)
    for k, s in zip(ks, ss):
        p = jnp.exp2(s - m)
        l = l + jnp.sum(p, axis=-1, keepdims=True)
        acc = acc + _dot(p.astype(BF16), k)
    o_ref[...] = (acc / l).reshape(N_HEADS, s_len, KV_LORA)


def _attn_decode(q_lat, q_pe, ckv_s, kpe_s, cache_ckv, cache_kpe, page_table, *, layer, s_len):
    db, n_pages = page_table.shape
    page = cache_ckv.shape[2]
    assert s_len & (s_len - 1) == 0 and s_len <= LANES and page % LANES == 0
    buf_rows = n_pages * page + LANES
    cache_kpe_t = jnp.swapaxes(cache_kpe, 2, 3)
    kpe_new_t = jnp.pad(jnp.swapaxes(kpe_s.reshape(db, s_len, QK_ROPE), 1, 2), ((0, 0), (0, 0), (0, LANES - s_len)))
    qmap = lambda b, pt: (0, b, 0)
    grid_spec = pltpu.PrefetchScalarGridSpec(
        num_scalar_prefetch=1,
        grid=(db,),
        in_specs=[pl.BlockSpec((N_HEADS, s_len, KV_LORA), qmap), pl.BlockSpec((N_HEADS, s_len, QK_ROPE), qmap),
                  pl.BlockSpec((s_len, KV_LORA), lambda b, pt: (b, 0)),
                  pl.BlockSpec((1, QK_ROPE, LANES), lambda b, pt: (b, 0, 0)),
                  pl.BlockSpec(memory_space=pl.ANY), pl.BlockSpec(memory_space=pl.ANY)],
        out_specs=pl.BlockSpec((N_HEADS, s_len, KV_LORA), qmap),
        scratch_shapes=[pltpu.VMEM((2, buf_rows, KV_LORA), F32), pltpu.VMEM((2, QK_ROPE, buf_rows), F32),
                        pltpu.SemaphoreType.DMA((2, 2))],
    )
    return pl.pallas_call(
        functools.partial(_attn_decode_kernel, layer=layer, n_pages=n_pages, page=page, s_len=s_len),
        grid_spec=grid_spec,
        out_shape=jax.ShapeDtypeStruct((N_HEADS, db * s_len, KV_LORA), F32),
        compiler_params=_params(("arbitrary",)),
        name="attn_decode",
    )(page_table, q_lat, q_pe, ckv_s, kpe_new_t, cache_ckv, cache_kpe_t)


def _attn_out_kernel(o_ref, x_ref, wuv_ref, wo_ref, out_ref):
    pieces = []
    for p in range(N_HEADS // 2):
        pieces.append(_dot(o_ref[2 * p].astype(BF16), wuv_ref[2 * p])
                      + _dot(o_ref[2 * p + 1].astype(BF16), wuv_ref[2 * p + 1]))
    o = jnp.concatenate(pieces, axis=-1).astype(BF16)
    out_ref[...] = x_ref[...] + _dot(o, wo_ref[...])


def _attn_out(o_lat, x, w_uv2, w_o, *, tm):
    n, d = x.shape
    assert n % tm == 0
    return pl.pallas_call(
        _attn_out_kernel,
        grid=(n // tm,),
        in_specs=[pl.BlockSpec((N_HEADS, tm, KV_LORA), lambda i: (0, i, 0)), pl.BlockSpec((tm, d), lambda i: (i, 0)),
                  _const_spec(w_uv2.shape), _const_spec(w_o.shape)],
        out_specs=pl.BlockSpec((tm, d), lambda i: (i, 0)),
        out_shape=jax.ShapeDtypeStruct((n, d), F32),
        compiler_params=_params(("parallel",)),
        name="attn_out",
    )(o_lat, x, w_uv2, w_o)


SUB = 8


def _load_rows(ref, n):
    return jnp.concatenate([ref[pl.ds(c, n, stride=SUB), :] for c in range(SUB)], axis=-1)


def _store_rows(ref, x):
    n = x.shape[0]
    for c in range(SUB):
        ref[pl.ds(c, n, stride=SUB), :] = x[:, c * LANES:(c + 1) * LANES]


def _router_kernel(x_ref, g_ref, whi_ref, wlo_ref, b_ref, h_ref, idx_ref, gate_ref):
    h = _rms(x_ref[...], g_ref[...])
    _store_rows(h_ref, h)
    h_hi = h.astype(BF16)
    h_lo = (h - h_hi.astype(F32)).astype(BF16)
    logits = _dot(h_hi, whi_ref[...]) + _dot(h_lo, whi_ref[...]) + _dot(h_hi, wlo_ref[...])
    lane = lax.broadcasted_iota(jnp.int32, logits.shape, 1)
    lane_f = lane.astype(F32)
    sel = jnp.where(lane < N_EXPERTS, logits + b_ref[...], NEG)
    m1 = jnp.max(sel, axis=-1, keepdims=True)
    i1 = jnp.min(jnp.where(sel == m1, lane_f, float(LANES)), axis=-1, keepdims=True)
    sel2 = jnp.where(lane_f == i1, NEG, sel)
    m2 = jnp.max(sel2, axis=-1, keepdims=True)
    i2 = jnp.min(jnp.where(sel2 == m2, lane_f, float(LANES)), axis=-1, keepdims=True)
    l1 = jnp.sum(jnp.where(lane_f == i1, logits, 0.0), axis=-1, keepdims=True)
    l2 = jnp.sum(jnp.where(lane_f == i2, logits, 0.0), axis=-1, keepdims=True)
    mx = jnp.maximum(l1, l2)
    e1 = jnp.exp(l1 - mx)
    e2 = jnp.exp(l2 - mx)
    den = e1 + e2
    idx_ref[...] = jnp.where(lane == 0, i1, jnp.where(lane == 1, i2, 0.0)).astype(jnp.int32)
    gate_ref[...] = jnp.where(lane == 0, e1 / den, jnp.where(lane == 1, e2 / den, 0.0))


def _router(x, g, w_hi, w_lo, bias, *, tm):
    n, d = x.shape
    assert n % tm == 0 and d == SUB * LANES
    tok = lambda width: pl.BlockSpec((tm, width), lambda i: (i, 0))
    return pl.pallas_call(
        _router_kernel,
        grid=(n // tm,),
        in_specs=[tok(d), _const_spec((1, d)), _const_spec(w_hi.shape), _const_spec(w_lo.shape),
                  _const_spec((1, LANES))],
        out_specs=[pl.BlockSpec((tm * SUB, LANES), lambda i: (i, 0)), tok(LANES), tok(LANES)],
        out_shape=[jax.ShapeDtypeStruct((n * SUB, LANES), F32), jax.ShapeDtypeStruct((n, LANES), jnp.int32),
                   jax.ShapeDtypeStruct((n, LANES), F32)],
        compiler_params=_params(("parallel",)),
        name="moe_router",
    )(x, g.reshape(1, d), w_hi, w_lo, bias)


def _dispatch_kernel(pos_ref, h_ref, init_ref, out_ref, sem, *, tm):
    del init_ref
    base = pl.program_id(0) * tm

    def start(r, c):
        src = h_ref.at[pl.ds(pl.multiple_of(r * SUB, SUB), SUB)]
        for k in range(TOP_K):
            dst = pl.multiple_of(pos_ref[TOP_K * (base + r) + k] * SUB, SUB)
            pltpu.make_async_copy(src, out_ref.at[pl.ds(dst, SUB)], sem).start()
        return c

    lax.fori_loop(0, tm, start, 0, unroll=8)
    for k in range(TOP_K):
        pltpu.make_async_copy(h_ref, out_ref.at[pl.ds(0, tm * SUB)], sem).wait()


def _dispatch(pos, h, n_rows, *, tm):
    n = h.shape[0] // SUB
    assert n % tm == 0
    grid_spec = pltpu.PrefetchScalarGridSpec(
        num_scalar_prefetch=1,
        grid=(n // tm,),
        in_specs=[pl.BlockSpec((tm * SUB, LANES), lambda i, pos: (i, 0)), pl.BlockSpec(memory_space=pl.ANY)],
        out_specs=pl.BlockSpec(memory_space=pl.ANY),
        scratch_shapes=[pltpu.SemaphoreType.DMA(())],
    )
    return pl.pallas_call(
        functools.partial(_dispatch_kernel, tm=tm),
        grid_spec=grid_spec,
        out_shape=jax.ShapeDtypeStruct((n_rows * SUB, LANES), F32),
        input_output_aliases={2: 0},
        compiler_params=_params(("arbitrary",)),
        name="moe_dispatch",
    )(pos, h, jnp.zeros((n_rows * SUB, LANES), F32))


def _expert_kernel(te_ref, x_ref, wg_ref, wu_ref, wd_ref, o_ref, acc_ref, *, n_tiles):
    t = pl.program_id(0)

    tm = acc_ref.shape[0]

    @pl.when(t < te_ref[n_tiles])
    def _():
        _swiglu_into(acc_ref, _load_rows(x_ref, tm).astype(BF16), wg_ref, wu_ref, wd_ref, lead=(0,))
        _store_rows(o_ref, acc_ref[...])

    @pl.when(t >= te_ref[n_tiles])
    def _():
        o_ref[...] = jnp.zeros(o_ref.shape, F32)


def _expert_ffn(te, xs, wg, wu, wd, *, tm):
    r = xs.shape[0] // SUB
    n_tiles = r // tm
    wmap = lambda t, te: (te[t], 0, 0)
    grid_spec = pltpu.PrefetchScalarGridSpec(
        num_scalar_prefetch=1,
        grid=(n_tiles,),
        in_specs=[pl.BlockSpec((tm * SUB, LANES), lambda t, te: (t, 0)),
                  pl.BlockSpec((1,) + wg.shape[1:], wmap), pl.BlockSpec((1,) + wu.shape[1:], wmap),
                  pl.BlockSpec((1,) + wd.shape[1:], wmap)],
        out_specs=pl.BlockSpec((tm * SUB, LANES), lambda t, te: (t, 0)),
        scratch_shapes=[pltpu.VMEM((tm, SUB * LANES), F32)],
    )
    return pl.pallas_call(
        functools.partial(_expert_kernel, n_tiles=n_tiles),
        grid_spec=grid_spec,
        out_shape=jax.ShapeDtypeStruct(xs.shape, F32),
        compiler_params=_params(("arbitrary",)),
        name="moe_experts",
    )(te, xs, wg, wu, wd)


def _combine_kernel(pos_ref, x_ref, gate_ref, gf_ref, y_ref, o_ref, buf, sem, *, tm, final_norm):
    base = pl.program_id(0) * tm

    def start(r, c):
        row = pl.multiple_of(r * SUB, SUB)
        for k in range(TOP_K):
            src = pl.multiple_of(pos_ref[TOP_K * (base + r) + k] * SUB, SUB)
            pltpu.make_async_copy(y_ref.at[pl.ds(src, SUB)], buf.at[k, pl.ds(row, SUB)], sem).start()
        return c

    lax.fori_loop(0, tm, start, 0, unroll=8)
    for k in range(TOP_K):
        pltpu.make_async_copy(y_ref.at[pl.ds(0, tm * SUB)], buf.at[k], sem).wait()
    gate = gate_ref[...]
    out = x_ref[...] + gate[:, 0:1] * _load_rows(buf.at[0], tm) + gate[:, 1:2] * _load_rows(buf.at[1], tm)
    if final_norm:
        out = _rms(out, gf_ref[...])
    o_ref[...] = out


def _combine(pos, x, gates, g_final, y, *, tm, final_norm):
    n, d = x.shape
    assert n % tm == 0
    grid_spec = pltpu.PrefetchScalarGridSpec(
        num_scalar_prefetch=1,
        grid=(n // tm,),
        in_specs=[pl.BlockSpec((tm, d), lambda i, pos: (i, 0)), pl.BlockSpec((tm, LANES), lambda i, pos: (i, 0)),
                  pl.BlockSpec((1, d), lambda i, pos: (0, 0)), pl.BlockSpec(memory_space=pl.ANY)],
        out_specs=pl.BlockSpec((tm, d), lambda i, pos: (i, 0)),
        scratch_shapes=[pltpu.VMEM((TOP_K, tm * SUB, LANES), F32), pltpu.SemaphoreType.DMA(())],
    )
    return pl.pallas_call(
        functools.partial(_combine_kernel, tm=tm, final_norm=final_norm),
        grid_spec=grid_spec,
        out_shape=jax.ShapeDtypeStruct((n, d), F32),
        compiler_params=_params(("arbitrary",)),
        name="moe_combine",
    )(pos, x, gates, g_final.reshape(1, d), y)


def _route_positions(idx, tm, n_tiles):
    e = idx[:, :TOP_K].reshape(-1)
    onehot = (e[:, None] == jnp.arange(N_EXPERTS, dtype=jnp.int32)[None, :]).astype(jnp.int32)
    csum = jnp.cumsum(onehot, axis=0)
    rank = jnp.sum(csum * onehot, axis=1) - 1
    counts = csum[-1]
    padded = (counts + tm - 1) // tm * tm
    ends = jnp.cumsum(padded)
    pos = (ends - padded)[e] + rank
    n_used = ends[-1] // tm
    starts = jnp.arange(n_tiles, dtype=jnp.int32) * tm
    te = jnp.sum((starts[:, None] >= ends[None, :]).astype(jnp.int32), axis=1)
    te = jnp.minimum(te, jnp.take(te, jnp.maximum(n_used - 1, 0)))
    return pos.astype(jnp.int32), jnp.concatenate([te, n_used[None]]).astype(jnp.int32)


def _moe(x, g, w, g_final, *, tm, tile, final_norm):
    n, _ = x.shape
    h, idx, gates = _router(x, g, w["r_hi"], w["r_lo"], w["r_b"], tm=tm)
    n_tiles = -(-TOP_K * n // tile) + N_EXPERTS
    pos, te = _route_positions(idx, tile, n_tiles)
    xs = _dispatch(pos, h, n_tiles * tile, tm=tm)
    ys = _expert_ffn(te, xs, w["wg"], w["wu"], w["wd"], tm=tile)
    return _combine(pos, x, gates, g_final, ys, tm=tm, final_norm=final_norm)


def _rope_tables(pos, rows):
    half = QK_ROPE // 2
    freqs = 1.0 / (ROPE_BASE ** (jnp.arange(half, dtype=F32) * (2.0 / QK_ROPE)))
    ang = pos.astype(F32)[:, None] * freqs[None, :]
    cos, sin = jnp.cos(ang), jnp.sin(ang)
    cos = jnp.tile(jnp.concatenate([cos, cos], axis=-1), (rows // pos.shape[0], N_HEADS))
    sin = jnp.tile(jnp.concatenate([-sin, sin], axis=-1), (rows // pos.shape[0], N_HEADS))
    return cos, sin


def _mla_weights(w_in, g_q, g_kv, w_uq, w_uk, w_uv, w_o):
    half = QK_ROPE // 2
    swap = jnp.concatenate([jnp.arange(half, QK_ROPE), jnp.arange(half)])
    c2 = Q_LORA + KV_LORA
    w_in_ext = jnp.concatenate([w_in, w_in[:, c2:][:, swap]], axis=1)
    uq = w_uq.reshape(Q_LORA, N_HEADS, QK_NOPE + QK_ROPE)
    rope_cols = uq[:, :, QK_NOPE:]
    a = jnp.transpose(w_uk, (1, 2, 0)).reshape(N_HEADS // 4, 4, QK_NOPE, KV_LORA)
    w_uk4 = jnp.einsum("pjnc,jk->pjnkc", a, jnp.eye(4, dtype=a.dtype)).reshape(N_HEADS // 4, 4 * QK_NOPE, 4 * KV_LORA)
    side = jax.nn.one_hot(jnp.arange(N_HEADS) % 2, 2, dtype=w_uv.dtype)
    w_uv2 = jnp.einsum("chv,hk->hckv", w_uv, side).reshape(N_HEADS, KV_LORA, 2 * V_HEAD)
    return {
        "w_in": w_in_ext.astype(BF16), "g_q": g_q.reshape(1, -1), "g_kv": g_kv.reshape(1, -1),
        "w_qn": uq[:, :, :QK_NOPE].reshape(Q_LORA, -1).astype(BF16),
        "w_qa": rope_cols.reshape(Q_LORA, -1).astype(BF16),
        "w_qb": rope_cols[:, :, swap].reshape(Q_LORA, -1).astype(BF16),
        "w_uk4": w_uk4.astype(BF16), "w_uv2": w_uv2.astype(BF16), "w_o": w_o.astype(BF16),
    }


def _moe_weights(w_router, b_router, w_gate, w_up, w_down):
    d = w_router.shape[0]
    wr = jnp.zeros((d, LANES), F32).at[:, :N_EXPERTS].set(w_router)
    hi = wr.astype(BF16)
    return {
        "r_hi": hi, "r_lo": (wr - hi.astype(F32)).astype(BF16),
        "r_b": jnp.zeros((1, LANES), F32).at[0, :N_EXPERTS].set(b_router),
        "wg": w_gate.astype(BF16), "wu": w_up.astype(BF16), "wd": w_down.astype(BF16),
    }


def _tile(n, want):
    if n <= want:
        return n
    t = want - want % 16
    while n % t:
        t -= 16
    return t


def kernel(x_prompt, x_sample, cache_ckv, cache_kpe, state_pool, page_table, meta_tokens, g_mix, g_ffn, g_final, pool_w, pool_scale, mla_w_in, mla_g_q, mla_g_kv, mla_w_uq, mla_w_uk, mla_w_uv, mla_w_o, ffn_w_gate, ffn_w_up, ffn_w_down, moe_w_router, moe_b_router, moe_w_gate, moe_w_up, moe_w_down):
    b, seq, d = x_prompt.shape
    db, s_len, _ = x_sample.shape
    depth = g_mix.shape[0]
    assert depth % 2 == 0
    past_len = page_table.shape[1] * cache_ckv.shape[2]
    n_main, n_samp = b * seq, db * s_len

    xm = x_prompt
    xs = x_sample
    xe = meta_tokens[None].astype(F32)

    tm_main = _tile(n_main, 1024)
    tm_samp = _tile(n_samp, 512)
    cos_m, sin_m = _rope_tables(N_META + jnp.arange(seq), max(seq, tm_main))
    cos_s, sin_s = _rope_tables(past_len + jnp.arange(s_len), tm_samp)
    cos_e, sin_e = _rope_tables(jnp.arange(N_META), N_META)

    outs = {k: [] for k in ("ckv_p", "kpe_p", "pool_p", "ckv_s", "kpe_s", "pool_s")}
    for i in range(depth):
        j = i // 2
        last = i == depth - 1
        if i % 2 == 0:
            w_bf = pool_w[j].astype(BF16)
            zeros_e = jnp.zeros((1, HIST, d), F32)
            xe, tail_e = _pool_layer(xe, zeros_e, g_mix[i], w_bf, pool_scale[j], has_hist=False, tb=1, tl=N_META)
            xm, tail_m = _pool_layer(xm, tail_e, g_mix[i], w_bf, pool_scale[j], has_hist=True, tb=1,
                                     tl=min(512, seq))
            prev_s = jnp.concatenate([jnp.zeros((db, 1, d), F32), state_pool[j]], axis=1)
            xs, tail_s = _pool_layer(xs, prev_s, g_mix[i], w_bf, pool_scale[j], has_hist=True,
                                     tb=_tile(db, 64), tl=s_len)
            outs["pool_p"].append(tail_m[:, 1:])
            outs["pool_s"].append(tail_s[:, 1:])

            wg, wu, wd = ffn_w_gate[j].astype(BF16), ffn_w_up[j].astype(BF16), ffn_w_down[j].astype(BF16)
            xm = _ffn_dense(xm.reshape(n_main, d), g_ffn[i], wg, wu, wd, tm=tm_main).reshape(b, seq, d)
            small = jnp.concatenate([xs.reshape(n_samp, d), xe[0]], axis=0)
            small = _ffn_dense(small, g_ffn[i], wg, wu, wd, tm=small.shape[0])
            xs, xe = small[:n_samp].reshape(db, s_len, d), small[n_samp:][None]
        else:
            w = _mla_weights(mla_w_in[j], mla_g_q[j], mla_g_kv[j], mla_w_uq[j], mla_w_uk[j], mla_w_uv[j], mla_w_o[j])
            xe2, xm2, xs2 = xe[0], xm.reshape(n_main, d), xs.reshape(n_samp, d)
            ql_e, qp_e, ckv_e, kpe_e, ckvb_e, kpeb_e = _mla_proj(xe2, cos_e, sin_e, g_mix[i], w, tm=N_META, q_dtype=BF16)
            ql_m, qp_m, ckv_m, kpe_m, ckvb_m, kpeb_m = _mla_proj(xm2, cos_m, sin_m, g_mix[i], w, tm=tm_main, q_dtype=BF16)
            ql_s, qp_s, ckv_s, kpe_s, _, _ = _mla_proj(xs2, cos_s, sin_s, g_mix[i], w, tm=tm_samp, q_dtype=F32)

            o_e = _attn_causal(ql_e, qp_e, ckvb_e, kpeb_e, None, batch=1, tqb=N_META, tqs=N_META, tk=LANES)
            o_m = _attn_causal(ql_m, qp_m, ckvb_m, kpeb_m, (ckvb_e, kpeb_e), batch=b, tqb=min(256, seq), tqs=64,
                               tk=512)
            o_s = _attn_decode(ql_s, qp_s, ckv_s, kpe_s, cache_ckv, cache_kpe, page_table, layer=j, s_len=s_len)

            xe2 = _attn_out(o_e, xe2, w["w_uv2"], w["w_o"], tm=N_META)
            xm2 = _attn_out(o_m, xm2, w["w_uv2"], w["w_o"], tm=tm_main)
            xs2 = _attn_out(o_s, xs2, w["w_uv2"], w["w_o"], tm=tm_samp)

            outs["ckv_p"].append(jnp.concatenate(
                [jnp.broadcast_to(ckv_e[None], (b, N_META, KV_LORA)), ckv_m.reshape(b, seq, KV_LORA)], axis=1))
            outs["kpe_p"].append(jnp.concatenate(
                [jnp.broadcast_to(kpe_e[None], (b, N_META, QK_ROPE)), kpe_m.reshape(b, seq, QK_ROPE)], axis=1))
            outs["ckv_s"].append(ckv_s.reshape(db, s_len, KV_LORA))
            outs["kpe_s"].append(kpe_s.reshape(db, s_len, QK_ROPE))

            mw = _moe_weights(moe_w_router[j], moe_b_router[j], moe_w_gate[j], moe_w_up[j], moe_w_down[j])
            xm = _moe(xm2, g_ffn[i], mw, g_final, tm=_tile(n_main, 512), tile=512, final_norm=last).reshape(b, seq, d)
            small = jnp.concatenate([xs2, xe2], axis=0)
            small = _moe(small, g_ffn[i], mw, g_final, tm=small.shape[0], tile=128, final_norm=last)
            xs, xe = small[:n_samp].reshape(db, s_len, d), small[n_samp:][None]

    return (xm, xs, jnp.stack(outs["ckv_p"]), jnp.stack(outs["kpe_p"]), jnp.stack(outs["pool_p"]),
            jnp.stack(outs["ckv_s"]), jnp.stack(outs["kpe_s"]), jnp.stack(outs["pool_s"]))
```

```python
import functools

import jax
import jax.numpy as jnp
from jax import lax
from jax.experimental import pallas as pl
from jax.experimental.pallas import tpu as pltpu

EPS = 1e-6
N_META = 16
POOL_WINDOWS = (2, 4, 8, 16)
N_HEADS = 16
Q_LORA = 384
KV_LORA = 256
QK_NOPE = 64
QK_ROPE = 32
V_HEAD = 64
ROPE_BASE = 10000.0
ATTN_SCALE = (QK_NOPE + QK_ROPE) ** -0.5
Q_SCALE = ATTN_SCALE * 1.4426950408889634
N_EXPERTS = 8
TOP_K = 2

HIST = 16
LANES = 128
NEG = -1e30
VMEM_LIMIT = 56 * 1024 * 1024
BF16 = jnp.bfloat16
F32 = jnp.float32


def _dot(a, b):
    return jnp.dot(a, b, preferred_element_type=F32)


def _dot_nt(a, b):
    return lax.dot_general(a, b, (((1,), (1,)), ((), ())), preferred_element_type=F32)


def _rms(x, g):
    return x * lax.rsqrt(jnp.mean(x * x, axis=-1, keepdims=True) + EPS) * g


def _params(sem, vmem=VMEM_LIMIT):
    return pltpu.CompilerParams(dimension_semantics=sem, vmem_limit_bytes=vmem)


def _const_spec(shape):
    nd = len(shape)
    return pl.BlockSpec(shape, lambda *_: (0,) * nd)


def _pool_kernel(*refs, has_hist, multi_tile, tb, tl):
    if multi_tile:
        x_ref, halo_ref, prev_ref, g_ref, w_ref, sc_ref, o_ref, tail_ref, hext_ref = refs
    else:
        x_ref, prev_ref, g_ref, w_ref, sc_ref, o_ref, tail_ref, hext_ref = refs
        halo_ref = None
    t = pl.program_id(1)
    d = x_ref.shape[-1]
    gc = d // len(POOL_WINDOWS)
    g = g_ref[...]
    x = x_ref[...]
    h = _rms(x, g)
    hext_ref[:, HIST:, :] = h
    prev = jnp.broadcast_to(prev_ref[...], (tb, HIST, d))
    if multi_tile:
        @pl.when(t == 0)
        def _():
            hext_ref[:, :HIST, :] = prev

        @pl.when(t > 0)
        def _():
            hext_ref[:, :HIST, :] = _rms(halo_ref[...], g)
    else:
        hext_ref[:, :HIST, :] = prev

    for gi, w in enumerate(POOL_WINDOWS):
        c0, c1 = gi * gc, (gi + 1) * gc
        win = hext_ref[:, pl.ds(HIST, tl), c0:c1]
        for i in range(1, w):
            win = win + hext_ref[:, pl.ds(HIST - i, tl), c0:c1]
        hg = hext_ref[:, pl.ds(HIST, tl), c0:c1]
        if has_hist:
            dg = win * (1.0 / w) - hg
        else:
            row = lax.broadcasted_iota(jnp.int32, (tb, tl, gc), 1) + t * tl
            cnt = jnp.minimum(row + 1, w).astype(F32)
            dg = win / cnt - hg
        dg = dg.reshape(tb * tl, gc).astype(BF16)
        yg = _dot(dg, w_ref[gi]).reshape(tb, tl, gc)
        o_ref[:, :, c0:c1] = x_ref[:, :, c0:c1] + yg * sc_ref[:, c0:c1]
    tail_ref[...] = hext_ref[:, pl.ds(tl, HIST), :]


def _pool_layer(x, prev, g, w_bf, scale, *, has_hist, tb, tl):
    b, l, d = x.shape
    n_t = l // tl
    multi = n_t > 1
    assert l % tl == 0 and b % tb == 0 and (not multi or (tb == 1 and tl % HIST == 0))
    shared = prev.shape[0] == 1
    per_tile = tl // HIST
    in_specs = [pl.BlockSpec((tb, tl, d), lambda i, t: (i, t, 0))]
    args = [x]
    if multi:
        in_specs.append(pl.BlockSpec((tb, HIST, d), lambda i, t: (i, jnp.maximum(t * per_tile - 1, 0), 0)))
        args.append(x)
    in_specs += [
        pl.BlockSpec((1 if shared else tb, HIST, d), (lambda i, t: (0, 0, 0)) if shared else (lambda i, t: (i, 0, 0))),
        _const_spec((1, d)), _const_spec(w_bf.shape), _const_spec((1, d)),
    ]
    args += [prev, g.reshape(1, d), w_bf, scale.reshape(1, d)]
    return pl.pallas_call(
        functools.partial(_pool_kernel, has_hist=has_hist, multi_tile=multi, tb=tb, tl=tl),
        grid=(b // tb, n_t),
        in_specs=in_specs,
        out_specs=[pl.BlockSpec((tb, tl, d), lambda i, t: (i, t, 0)),
                   pl.BlockSpec((tb, HIST, d), lambda i, t: (i, 0, 0))],
        out_shape=[jax.ShapeDtypeStruct((b, l, d), F32), jax.ShapeDtypeStruct((b, HIST, d), F32)],
        scratch_shapes=[pltpu.VMEM((tb, tl + HIST, d), F32)],
        compiler_params=_params(("parallel", "arbitrary")),
        name="pool_mixer",
    )(*args)


FF_CHUNK = 256


def _swiglu_into(acc_ref, h, wg_ref, wu_ref, wd_ref, lead=()):
    n_ff = wg_ref.shape[-1]
    for c in range(0, n_ff, FF_CHUNK):
        gg = _dot(h, wg_ref[lead + (slice(None), slice(c, c + FF_CHUNK))])
        uu = _dot(h, wu_ref[lead + (slice(None), slice(c, c + FF_CHUNK))])
        a = (gg * jax.nn.sigmoid(gg) * uu).astype(BF16)
        y = _dot(a, wd_ref[lead + (slice(c, c + FF_CHUNK), slice(None))])
        if c == 0:
            acc_ref[...] = y
        else:
            acc_ref[...] += y


def _ffn_kernel(x_ref, g_ref, wg_ref, wu_ref, wd_ref, o_ref, acc_ref):
    x = x_ref[...]
    h = _rms(x, g_ref[...]).astype(BF16)
    _swiglu_into(acc_ref, h, wg_ref, wu_ref, wd_ref)
    o_ref[...] = x + acc_ref[...]


def _ffn_dense(x, g, wg, wu, wd, *, tm):
    n, d = x.shape
    assert n % tm == 0
    resident = lambda shape: pl.BlockSpec(shape, lambda i: (0, 0), pipeline_mode=pl.Buffered(1))
    return pl.pallas_call(
        _ffn_kernel,
        grid=(n // tm,),
        in_specs=[pl.BlockSpec((tm, d), lambda i: (i, 0)), _const_spec((1, d)),
                  resident(wg.shape), resident(wu.shape), resident(wd.shape)],
        out_specs=pl.BlockSpec((tm, d), lambda i: (i, 0)),
        out_shape=jax.ShapeDtypeStruct((n, d), F32),
        scratch_shapes=[pltpu.VMEM((tm, d), F32)],
        compiler_params=_params(("parallel",)),
        name="ffn_dense",
    )(x, g.reshape(1, d), wg, wu, wd)


def _mla_proj_kernel(x_ref, cos_ref, sin_ref, g_ref, win_ref, gq_ref, gkv_ref, wqn_ref, wqa_ref, wqb_ref, wuk_ref,
                     qlat_ref, qpe_ref, ckv_ref, kpe_ref, ckvb_ref, kpeb_ref):
    h = _rms(x_ref[...], g_ref[...]).astype(BF16)
    proj = _dot(h, win_ref[...])
    c1, c2, c3 = Q_LORA, Q_LORA + KV_LORA, Q_LORA + KV_LORA + QK_ROPE
    cqn = _rms(proj[:, :c1], gq_ref[...]).astype(BF16)
    ckv = _rms(proj[:, c1:c2], gkv_ref[...])
    cos = cos_ref[...]
    sin = sin_ref[...]
    kpe = proj[:, c2:c3] * cos[:, :QK_ROPE] + proj[:, c3:c3 + QK_ROPE] * sin[:, :QK_ROPE]
    ckv_ref[...] = ckv
    kpe_ref[...] = kpe
    ckvb_ref[...] = ckv.astype(BF16)
    kpeb_ref[...] = kpe.astype(BF16)

    qn = _dot(cqn, wqn_ref[...]).astype(BF16)
    quad = 4 * QK_NOPE
    for p in range(N_HEADS // 4):
        ql = _dot(qn[:, p * quad:(p + 1) * quad], wuk_ref[p]) * Q_SCALE
        for j in range(4):
            qlat_ref[4 * p + j] = ql[:, j * KV_LORA:(j + 1) * KV_LORA].astype(qlat_ref.dtype)
    qr = (_dot(cqn, wqa_ref[...]) * cos + _dot(cqn, wqb_ref[...]) * sin) * Q_SCALE
    for hd in range(N_HEADS):
        qpe_ref[hd] = qr[:, hd * QK_ROPE:(hd + 1) * QK_ROPE].astype(qpe_ref.dtype)


def _mla_proj(x, cos, sin, g, w, *, tm, q_dtype):
    n, d = x.shape
    assert n % tm == 0 and cos.shape[0] % tm == 0
    n_tab = cos.shape[0] // tm
    hr = N_HEADS * QK_ROPE
    tok = lambda width: pl.BlockSpec((tm, width), lambda i: (i, 0))
    tab = pl.BlockSpec((tm, hr), lambda i: (i % n_tab, 0))
    headed = lambda width: pl.BlockSpec((N_HEADS, tm, width), lambda i: (0, i, 0))
    return pl.pallas_call(
        _mla_proj_kernel,
        grid=(n // tm,),
        in_specs=[tok(d), tab, tab, _const_spec((1, d)), _const_spec(w["w_in"].shape),
                  _const_spec((1, Q_LORA)), _const_spec((1, KV_LORA)), _const_spec(w["w_qn"].shape),
                  _const_spec(w["w_qa"].shape), _const_spec(w["w_qb"].shape), _const_spec(w["w_uk4"].shape)],
        out_specs=[headed(KV_LORA), headed(QK_ROPE), tok(KV_LORA), tok(QK_ROPE), tok(KV_LORA), tok(QK_ROPE)],
        out_shape=[jax.ShapeDtypeStruct((N_HEADS, n, KV_LORA), q_dtype),
                   jax.ShapeDtypeStruct((N_HEADS, n, QK_ROPE), q_dtype),
                   jax.ShapeDtypeStruct((n, KV_LORA), F32), jax.ShapeDtypeStruct((n, QK_ROPE), F32),
                   jax.ShapeDtypeStruct((n, KV_LORA), BF16), jax.ShapeDtypeStruct((n, QK_ROPE), BF16)],
        compiler_params=_params(("parallel",)),
        name="mla_proj",
    )(x, cos, sin, g.reshape(1, d), w["w_in"], w["g_q"], w["g_kv"], w["w_qn"], w["w_qa"], w["w_qb"], w["w_uk4"])


ATTN_SPLIT = 2
ATTN_EDGE = 256
ROW_CHUNK = 32


def _attn_kernel(*refs, n_pre, t_len, tqb, tqs, tk):
    if n_pre:
        q_ref, qr_ref, k_ref, kr_ref, pk_ref, pkr_ref, o_ref, kf, krf, m_ref, l_ref, a_ref, acc_ref, s_ref, p_ref = refs
    else:
        q_ref, qr_ref, k_ref, kr_ref, o_ref, kf, krf, m_ref, l_ref, a_ref, acc_ref, s_ref, p_ref = refs
    qi = pl.program_id(1)
    n_keys = n_pre + t_len
    hg = N_HEADS // ATTN_SPLIT
    rows = hg * tqs
    tk_shift = tk.bit_length() - 1
    te = min(tk, ATTN_EDGE)
    te_shift = te.bit_length() - 1

    @pl.when(qi == 0)
    def _():
        if n_pre:
            kf[0:n_pre, :] = pk_ref[...]
            krf[0:n_pre, :] = pkr_ref[...]
        kf[n_pre:n_keys, :] = k_ref[...]
        krf[n_pre:n_keys, :] = kr_ref[...]
        if kf.shape[0] > n_keys:
            kf[n_keys:, :] = jnp.zeros((kf.shape[0] - n_keys, KV_LORA), kf.dtype)
            krf[n_keys:, :] = jnp.zeros((kf.shape[0] - n_keys, QK_ROPE), krf.dtype)

    def sub_block(i, carry):
        t0 = pl.multiple_of(i * tqs, tqs)
        p0 = n_pre + qi * tqb + i * tqs
        for g in range(ATTN_SPLIT):
            m_ref[g] = jnp.full((rows, LANES), NEG, F32)
            l_ref[g] = jnp.zeros((rows, LANES), F32)
            acc_ref[g] = jnp.zeros((rows, KV_LORA), F32)

        def update(k0, width, masked):
            k = kf[pl.ds(k0, width), :]
            kr = krf[pl.ds(k0, width), :]
            n_rep = width // LANES
            chunks = [slice(c, c + ROW_CHUNK) for c in range(0, rows, ROW_CHUNK)]
            for g in range(ATTN_SPLIT):
                q = q_ref[g * hg:(g + 1) * hg, pl.ds(t0, tqs), :].reshape(rows, KV_LORA)
                qr = qr_ref[g * hg:(g + 1) * hg, pl.ds(t0, tqs), :].reshape(rows, QK_ROPE)
                s = _dot_nt(q, k) + _dot_nt(qr, kr)
                if masked:
                    r = lax.broadcasted_iota(jnp.int32, s.shape, 0) & (tqs - 1)
                    c = lax.broadcasted_iota(jnp.int32, s.shape, 1)
                    s = jnp.where(c - r <= p0 - k0, s, NEG)
                s_ref[g, :, :width] = s
            for g in range(ATTN_SPLIT):
                for rs in chunks:
                    m_old = m_ref[g, rs, :]
                    row_max = jnp.max(s_ref[g, rs, :width], axis=-1, keepdims=True)
                    m_new = jnp.maximum(m_old, jnp.broadcast_to(row_max, (ROW_CHUNK, LANES)))
                    a_ref[g, rs, :] = jnp.exp2(m_old - m_new)
                    m_ref[g, rs, :] = m_new
                for rs in chunks:
                    p = jnp.exp2(s_ref[g, rs, :width] - jnp.concatenate([m_ref[g, rs, :]] * n_rep, axis=-1))
                    part = p[:, :LANES]
                    for c in range(1, n_rep):
                        part = part + p[:, c * LANES:(c + 1) * LANES]
                    l_ref[g, rs, :] = a_ref[g, rs, :] * l_ref[g, rs, :] + part
                    p_ref[g, rs, :width] = p.astype(BF16)
            for g in range(ATTN_SPLIT):
                alpha = jnp.concatenate([a_ref[g]] * (KV_LORA // LANES), axis=-1)
                acc_ref[g] = alpha * acc_ref[g] + _dot(p_ref[g, :, :width], k)

        def full_tile(j, c):
            update(pl.multiple_of(j * tk, tk), tk, False)
            return c

        def edge_tile(j, c):
            update(pl.multiple_of(j * te, te), te, True)
            return c

        n_full = (p0 + 1) >> tk_shift
        lax.fori_loop(0, n_full, full_tile, 0)
        lax.fori_loop(n_full * (tk // te), ((p0 + tqs - 1) >> te_shift) + 1, edge_tile, 0)
        for g in range(ATTN_SPLIT):
            o = acc_ref[g] / jnp.sum(l_ref[g], axis=-1, keepdims=True)
            o_ref[g * hg:(g + 1) * hg, pl.ds(t0, tqs), :] = o.reshape(hg, tqs, KV_LORA).astype(o_ref.dtype)
        return carry

    lax.fori_loop(0, tqb // tqs, sub_block, 0)


def _attn_causal(q_lat, q_pe, ckv_b, kpe_b, prefix, *, batch, tqb, tqs, tk):
    n = ckv_b.shape[0]
    t_len = n // batch
    n_pre = 0 if prefix is None else prefix[0].shape[0]
    assert t_len % tqb == 0 and tqb % tqs == 0 and tk % tqs == 0 and tk & (tk - 1) == 0 and tqs & (tqs - 1) == 0
    assert tqs % 16 == 0 and n_pre % 16 == 0 and (n_pre + t_len) % 16 == 0
    n_q = t_len // tqb
    key_rows = -(-(n_pre + t_len) // tk) * tk
    qmap = lambda b, qi: (0, b * n_q + qi, 0)
    kmap = lambda b, qi: (b, 0)
    in_specs = [pl.BlockSpec((N_HEADS, tqb, KV_LORA), qmap), pl.BlockSpec((N_HEADS, tqb, QK_ROPE), qmap),
                pl.BlockSpec((t_len, KV_LORA), kmap), pl.BlockSpec((t_len, QK_ROPE), kmap)]
    args = [q_lat, q_pe, ckv_b, kpe_b]
    if n_pre:
        in_specs += [_const_spec(prefix[0].shape), _const_spec(prefix[1].shape)]
        args += list(prefix)
    rows = N_HEADS // ATTN_SPLIT * tqs
    return pl.pallas_call(
        functools.partial(_attn_kernel, n_pre=n_pre, t_len=t_len, tqb=tqb, tqs=tqs, tk=tk),
        grid=(batch, n_q),
        in_specs=in_specs,
        out_specs=pl.BlockSpec((N_HEADS, tqb, KV_LORA), qmap),
        out_shape=jax.ShapeDtypeStruct((N_HEADS, n, KV_LORA), q_lat.dtype),
        scratch_shapes=[pltpu.VMEM((key_rows, KV_LORA), BF16), pltpu.VMEM((key_rows, QK_ROPE), BF16),
                        pltpu.VMEM((ATTN_SPLIT, rows, LANES), F32), pltpu.VMEM((ATTN_SPLIT, rows, LANES), F32),
                        pltpu.VMEM((ATTN_SPLIT, rows, LANES), F32), pltpu.VMEM((ATTN_SPLIT, rows, KV_LORA), F32),
                        pltpu.VMEM((ATTN_SPLIT, rows, tk), F32), pltpu.VMEM((ATTN_SPLIT, rows, tk), BF16)],
        compiler_params=_params(("parallel", "arbitrary")),
        name="attn_causal",
    )(*args)


KEY_CHUNK = 2048


def _attn_decode_kernel(pt_ref, q_ref, qr_ref, ks_ref, krs_ref, cckv_ref, ckpe_ref, o_ref,
                        kbuf, rbuf, sem, *, layer, n_pages, page, s_len):
    b = pl.program_id(0)
    nb = pl.num_programs(0)
    slot = b % 2
    past = n_pages * page

    def copies(bb, sl, p):
        pg = pt_ref[bb, p]
        off = pl.multiple_of(p * page, page)
        return (pltpu.make_async_copy(cckv_ref.at[layer, pg], kbuf.at[sl, pl.ds(off, page)], sem.at[0, sl]),
                pltpu.make_async_copy(ckpe_ref.at[layer, pg], rbuf.at[sl, :, pl.ds(off, page)], sem.at[1, sl]))

    def fetch(bb, sl):
        def body(p, c):
            for cp in copies(bb, sl, p):
                cp.start()
            return c
        lax.fori_loop(0, n_pages, body, 0, unroll=8)

    @pl.when(b == 0)
    def _():
        kbuf[:, past:, :] = jnp.zeros((2, LANES, KV_LORA), F32)
        fetch(0, 0)

    @pl.when(b + 1 < nb)
    def _():
        fetch(b + 1, 1 - slot)

    pltpu.make_async_copy(kbuf.at[slot, pl.ds(0, past)], kbuf.at[slot, pl.ds(0, past)], sem.at[0, slot]).wait()
    pltpu.make_async_copy(rbuf.at[slot, :, pl.ds(0, past)], rbuf.at[slot, :, pl.ds(0, past)], sem.at[1, slot]).wait()

    kbuf[slot, past:past + s_len, :] = ks_ref[...]
    rbuf[slot, :, past:] = krs_ref[0]

    rows = N_HEADS * s_len
    q = q_ref[...].reshape(rows, KV_LORA).astype(BF16)
    qr = qr_ref[...].reshape(rows, QK_ROPE).astype(BF16)
    bounds = list(range(0, past, KEY_CHUNK)) + [past]
    sizes = [min(KEY_CHUNK, past - c) for c in bounds[:-1]] + [LANES]
    ks, ss = [], []
    for c0, sz in zip(bounds, sizes):
        k = kbuf[slot, c0:c0 + sz, :].astype(BF16)
        kr_t = rbuf[slot, :, c0:c0 + sz].astype(BF16)
        s = _dot_nt(q, k) + _dot(qr, kr_t)
        if c0 == past:
            r = lax.broadcasted_iota(jnp.int32, s.shape, 0) & (s_len - 1)
            c = lax.broadcasted_iota(jnp.int32, s.shape, 1)
            s = jnp.where(c <= r, s, NEG)
        ks.append(k)
        ss.append(s)
    m = functools.reduce(jnp.maximum, [jnp.max(s, axis=-1, keepdims=True) for s in ss])
    l = jnp.zeros((rows, 1), F32)
    acc = jnp.zeros((rows, KV_LORA), F32)
    for k, s in zip(ks, ss):
        p = jnp.exp2(s - m)
        l = l + jnp.sum(p, axis=-1, keepdims=True)
        acc = acc + _dot(p.astype(BF16), k)
    o_ref[...] = (acc / l).reshape(N_HEADS, s_len, KV_LORA)


def _attn_decode(q_lat, q_pe, ckv_s, kpe_s, cache_ckv, cache_kpe, page_table, *, layer, s_len):
    db, n_pages = page_table.shape
    page = cache_ckv.shape[2]
    assert s_len & (s_len - 1) == 0 and s_len <= LANES and page % LANES == 0
    buf_rows = n_pages * page + LANES
    cache_kpe_t = jnp.swapaxes(cache_kpe, 2, 3)
    kpe_new_t = jnp.pad(jnp.swapaxes(kpe_s.reshape(db, s_len, QK_ROPE), 1, 2), ((0, 0), (0, 0), (0, LANES - s_len)))
    qmap = lambda b, pt: (0, b, 0)
    grid_spec = pltpu.PrefetchScalarGridSpec(
        num_scalar_prefetch=1,
        grid=(db,),
        in_specs=[pl.BlockSpec((N_HEADS, s_len, KV_LORA), qmap), pl.BlockSpec((N_HEADS, s_len, QK_ROPE), qmap),
                  pl.BlockSpec((s_len, KV_LORA), lambda b, pt: (b, 0)),
                  pl.BlockSpec((1, QK_ROPE, LANES), lambda b, pt: (b, 0, 0)),
                  pl.BlockSpec(memory_space=pl.ANY), pl.BlockSpec(memory_space=pl.ANY)],
        out_specs=pl.BlockSpec((N_HEADS, s_len, KV_LORA), qmap),
        scratch_shapes=[pltpu.VMEM((2, buf_rows, KV_LORA), F32), pltpu.VMEM((2, QK_ROPE, buf_rows), F32),
                        pltpu.SemaphoreType.DMA((2, 2))],
    )
    return pl.pallas_call(
        functools.partial(_attn_decode_kernel, layer=layer, n_pages=n_pages, page=page, s_len=s_len),
        grid_spec=grid_spec,
        out_shape=jax.ShapeDtypeStruct((N_HEADS, db * s_len, KV_LORA), F32),
        compiler_params=_params(("arbitrary",)),
        name="attn_decode",
    )(page_table, q_lat, q_pe, ckv_s, kpe_new_t, cache_ckv, cache_kpe_t)


def _attn_out_kernel(o_ref, x_ref, wuv_ref, wo_ref, out_ref):
    pieces = []
    for p in range(N_HEADS // 2):
        pieces.append(_dot(o_ref[2 * p].astype(BF16), wuv_ref[2 * p])
                      + _dot(o_ref[2 * p + 1].astype(BF16), wuv_ref[2 * p + 1]))
    o = jnp.concatenate(pieces, axis=-1).astype(BF16)
    out_ref[...] = x_ref[...] + _dot(o, wo_ref[...])


def _attn_out(o_lat, x, w_uv2, w_o, *, tm):
    n, d = x.shape
    assert n % tm == 0
    return pl.pallas_call(
        _attn_out_kernel,
        grid=(n // tm,),
        in_specs=[pl.BlockSpec((N_HEADS, tm, KV_LORA), lambda i: (0, i, 0)), pl.BlockSpec((tm, d), lambda i: (i, 0)),
                  _const_spec(w_uv2.shape), _const_spec(w_o.shape)],
        out_specs=pl.BlockSpec((tm, d), lambda i: (i, 0)),
        out_shape=jax.ShapeDtypeStruct((n, d), F32),
        compiler_params=_params(("parallel",)),
        name="attn_out",
    )(o_lat, x, w_uv2, w_o)


SUB = 8


def _load_rows(ref, n):
    return jnp.concatenate([ref[pl.ds(c, n, stride=SUB), :] for c in range(SUB)], axis=-1)


def _store_rows(ref, x):
    n = x.shape[0]
    for c in range(SUB):
        ref[pl.ds(c, n, stride=SUB), :] = x[:, c * LANES:(c + 1) * LANES]


def _router_kernel(x_ref, g_ref, whi_ref, wlo_ref, b_ref, h_ref, idx_ref, gate_ref):
    h = _rms(x_ref[...], g_ref[...])
    _store_rows(h_ref, h)
    h_hi = h.astype(BF16)
    h_lo = (h - h_hi.astype(F32)).astype(BF16)
    logits = _dot(h_hi, whi_ref[...]) + _dot(h_lo, whi_ref[...]) + _dot(h_hi, wlo_ref[...])
    lane = lax.broadcasted_iota(jnp.int32, logits.shape, 1)
    lane_f = lane.astype(F32)
    sel = jnp.where(lane < N_EXPERTS, logits + b_ref[...], NEG)
    m1 = jnp.max(sel, axis=-1, keepdims=True)
    i1 = jnp.min(jnp.where(sel == m1, lane_f, float(LANES)), axis=-1, keepdims=True)
    sel2 = jnp.where(lane_f == i1, NEG, sel)
    m2 = jnp.max(sel2, axis=-1, keepdims=True)
    i2 = jnp.min(jnp.where(sel2 == m2, lane_f, float(LANES)), axis=-1, keepdims=True)
    l1 = jnp.sum(jnp.where(lane_f == i1, logits, 0.0), axis=-1, keepdims=True)
    l2 = jnp.sum(jnp.where(lane_f == i2, logits, 0.0), axis=-1, keepdims=True)
    mx = jnp.maximum(l1, l2)
    e1 = jnp.exp(l1 - mx)
    e2 = jnp.exp(l2 - mx)
    den = e1 + e2
    idx_ref[...] = jnp.where(lane == 0, i1, jnp.where(lane == 1, i2, 0.0)).astype(jnp.int32)
    gate_ref[...] = jnp.where(lane == 0, e1 / den, jnp.where(lane == 1, e2 / den, 0.0))


def _router(x, g, w_hi, w_lo, bias, *, tm):
    n, d = x.shape
    assert n % tm == 0 and d == SUB * LANES
    tok = lambda width: pl.BlockSpec((tm, width), lambda i: (i, 0))
    return pl.pallas_call(
        _router_kernel,
        grid=(n // tm,),
        in_specs=[tok(d), _const_spec((1, d)), _const_spec(w_hi.shape), _const_spec(w_lo.shape),
                  _const_spec((1, LANES))],
        out_specs=[pl.BlockSpec((tm * SUB, LANES), lambda i: (i, 0)), tok(LANES), tok(LANES)],
        out_shape=[jax.ShapeDtypeStruct((n * SUB, LANES), F32), jax.ShapeDtypeStruct((n, LANES), jnp.int32),
                   jax.ShapeDtypeStruct((n, LANES), F32)],
        compiler_params=_params(("parallel",)),
        name="moe_router",
    )(x, g.reshape(1, d), w_hi, w_lo, bias)


def _dispatch_kernel(pos_ref, pad_ref, h_ref, *rest, tm, first):
    if first:
        out_ref, zero_ref, sem, fill_sem = rest
        n_pad = pad_ref.shape[0]

        @pl.when(pl.program_id(0) == 0)
        def _():
            zero_ref[...] = jnp.zeros(zero_ref.shape, F32)

            def fill(r, c):
                dst = pl.multiple_of(pad_ref[r] * SUB, SUB)
                pltpu.make_async_copy(zero_ref, out_ref.at[pl.ds(dst, SUB)], fill_sem).start()
                return c

            lax.fori_loop(0, n_pad, fill, 0, unroll=8)
            pltpu.make_async_copy(out_ref.at[pl.ds(0, n_pad * SUB)], out_ref.at[pl.ds(0, n_pad * SUB)], fill_sem).wait()
    else:
        _, out_ref, sem = rest
    base = pl.program_id(0) * tm

    def start(r, c):
        src = h_ref.at[pl.ds(pl.multiple_of(r * SUB, SUB), SUB)]
        for k in range(TOP_K):
            dst = pl.multiple_of(pos_ref[TOP_K * (base + r) + k] * SUB, SUB)
            pltpu.make_async_copy(src, out_ref.at[pl.ds(dst, SUB)], sem).start()
        return c

    lax.fori_loop(0, tm, start, 0, unroll=8)
    for k in range(TOP_K):
        pltpu.make_async_copy(h_ref, out_ref.at[pl.ds(0, tm * SUB)], sem).wait()


def _dispatch(pos, pad_rows, h, sorted_rows, *, tm):
    n = h.shape[0] // SUB
    assert n % tm == 0
    first = not hasattr(sorted_rows, "dtype")
    h_spec = pl.BlockSpec((tm * SUB, LANES), lambda i, pos, pad: (i, 0))
    any_spec = pl.BlockSpec(memory_space=pl.ANY)
    grid_spec = pltpu.PrefetchScalarGridSpec(
        num_scalar_prefetch=2,
        grid=(n // tm,),
        in_specs=[h_spec] if first else [h_spec, any_spec],
        out_specs=any_spec,
        scratch_shapes=([pltpu.VMEM((SUB, LANES), F32), pltpu.SemaphoreType.DMA(()), pltpu.SemaphoreType.DMA(())]
                        if first else [pltpu.SemaphoreType.DMA(())]),
    )
    shape = (sorted_rows * SUB, LANES) if first else sorted_rows.shape
    return pl.pallas_call(
        functools.partial(_dispatch_kernel, tm=tm, first=first),
        grid_spec=grid_spec,
        out_shape=jax.ShapeDtypeStruct(shape, F32),
        input_output_aliases={} if first else {3: 0},
        compiler_params=_params(("arbitrary",)),
        name="moe_dispatch",
    )(*((pos, pad_rows, h) if first else (pos, pad_rows, h, sorted_rows)))


def _expert_kernel(te_ref, x_ref, wg_ref, wu_ref, wd_ref, o_ref, acc_ref, *, n_tiles):
    t = pl.program_id(0)

    tm = acc_ref.shape[0]

    @pl.when(t < te_ref[n_tiles])
    def _():
        _swiglu_into(acc_ref, _load_rows(x_ref, tm).astype(BF16), wg_ref, wu_ref, wd_ref, lead=(0, 0))
        _store_rows(o_ref, acc_ref[...])

    @pl.when(t >= te_ref[n_tiles])
    def _():
        o_ref[...] = jnp.zeros(o_ref.shape, F32)


def _expert_ffn(te, xs, wg, wu, wd, *, layer, n_tiles, tm):
    wmap = lambda t, te: (layer, te[t], 0, 0)
    xmap = lambda t, te: (jnp.minimum(t, te[n_tiles] - 1), 0)
    grid_spec = pltpu.PrefetchScalarGridSpec(
        num_scalar_prefetch=1,
        grid=(n_tiles,),
        in_specs=[pl.BlockSpec((tm * SUB, LANES), xmap),
                  pl.BlockSpec((1, 1) + wg.shape[2:], wmap), pl.BlockSpec((1, 1) + wu.shape[2:], wmap),
                  pl.BlockSpec((1, 1) + wd.shape[2:], wmap)],
        out_specs=pl.BlockSpec((tm * SUB, LANES), lambda t, te: (t, 0)),
        scratch_shapes=[pltpu.VMEM((tm, SUB * LANES), F32)],
    )
    return pl.pallas_call(
        functools.partial(_expert_kernel, n_tiles=n_tiles),
        grid_spec=grid_spec,
        out_shape=jax.ShapeDtypeStruct((n_tiles * tm * SUB, LANES), F32),
        compiler_params=_params(("arbitrary",)),
        name="moe_experts",
    )(te, xs, wg, wu, wd)


def _combine_kernel(pos_ref, x_ref, gate_ref, gf_ref, y_ref, o_ref, buf, sem, *, tm, final_norm):
    i = pl.program_id(0)
    slot = i % 2

    def gather(step, sl):
        base = step * tm

        def start(r, c):
            row = pl.multiple_of(r * SUB, SUB)
            for k in range(TOP_K):
                src = pl.multiple_of(pos_ref[TOP_K * (base + r) + k] * SUB, SUB)
                pltpu.make_async_copy(y_ref.at[pl.ds(src, SUB)], buf.at[sl, k, pl.ds(row, SUB)], sem.at[sl]).start()
            return c

        lax.fori_loop(0, tm, start, 0, unroll=8)

    @pl.when(i == 0)
    def _():
        gather(0, 0)

    @pl.when(i + 1 < pl.num_programs(0))
    def _():
        gather(i + 1, 1 - slot)

    for k in range(TOP_K):
        pltpu.make_async_copy(y_ref.at[pl.ds(0, tm * SUB)], buf.at[slot, k], sem.at[slot]).wait()
    gate = gate_ref[...]
    out = (x_ref[...] + gate[:, 0:1] * _load_rows(buf.at[slot, 0], tm)
           + gate[:, 1:2] * _load_rows(buf.at[slot, 1], tm))
    if final_norm:
        out = _rms(out, gf_ref[...])
    o_ref[...] = out


def _combine(pos, x, gates, g_final, y, *, tm, final_norm):
    n, d = x.shape
    assert n % tm == 0
    grid_spec = pltpu.PrefetchScalarGridSpec(
        num_scalar_prefetch=1,
        grid=(n // tm,),
        in_specs=[pl.BlockSpec((tm, d), lambda i, pos: (i, 0)), pl.BlockSpec((tm, LANES), lambda i, pos: (i, 0)),
                  pl.BlockSpec((1, d), lambda i, pos: (0, 0)), pl.BlockSpec(memory_space=pl.ANY)],
        out_specs=pl.BlockSpec((tm, d), lambda i, pos: (i, 0)),
        scratch_shapes=[pltpu.VMEM((2, TOP_K, tm * SUB, LANES), F32), pltpu.SemaphoreType.DMA((2,))],
    )
    return pl.pallas_call(
        functools.partial(_combine_kernel, tm=tm, final_norm=final_norm),
        grid_spec=grid_spec,
        out_shape=jax.ShapeDtypeStruct((n, d), F32),
        compiler_params=_params(("arbitrary",)),
        name="moe_combine",
    )(pos, x, gates, g_final.reshape(1, d), y)


def _route_positions(idx, tm, n_tiles):
    e = idx[:, :TOP_K].reshape(-1)
    onehot = (e[:, None] == jnp.arange(N_EXPERTS, dtype=jnp.int32)[None, :]).astype(jnp.int32)
    csum = jnp.cumsum(onehot, axis=0)
    rank = jnp.sum(csum * onehot, axis=1) - 1
    counts = csum[-1]
    padded = (counts + tm - 1) // tm * tm
    ends = jnp.cumsum(padded)
    pos = (ends - padded)[e] + rank
    n_used = ends[-1] // tm
    starts = jnp.arange(n_tiles, dtype=jnp.int32) * tm
    te = jnp.sum((starts[:, None] >= ends[None, :]).astype(jnp.int32), axis=1)
    te = jnp.minimum(te, jnp.take(te, jnp.maximum(n_used - 1, 0)))
    n_gap = n_tiles * tm - e.shape[0]
    gaps = jnp.concatenate([padded - counts, (n_tiles * tm - ends[-1])[None]])
    gap_start = jnp.concatenate([ends - padded + counts, ends[-1:]])
    gap_end = jnp.cumsum(gaps)
    j = jnp.arange(n_gap, dtype=jnp.int32)
    which = jnp.sum((j[:, None] >= gap_end[None, :]).astype(jnp.int32), axis=1)
    pad = jnp.take(gap_start, which) + j - jnp.take(gap_end - gaps, which)
    return (pos.astype(jnp.int32), pad.astype(jnp.int32), jnp.concatenate([te, n_used[None]]).astype(jnp.int32))


def _moe(streams, tms, g, w, experts, g_final, *, layer, tile, final_norm):
    routed = [_router(x, g, w["r_hi"], w["r_lo"], w["r_b"], tm=tm) for x, tm in zip(streams, tms)]
    n_all = sum(x.shape[0] for x in streams)
    n_tiles = -(-TOP_K * n_all // tile) + N_EXPERTS
    pos, pad_rows, te = _route_positions(jnp.concatenate([idx for _, idx, _ in routed], axis=0), tile, n_tiles)
    bounds = [0]
    for x in streams:
        bounds.append(bounds[-1] + TOP_K * x.shape[0])
    pos = [pos[a:b] for a, b in zip(bounds[:-1], bounds[1:])]
    xs = n_tiles * tile
    pad_rows = jnp.concatenate([pad_rows] + pos[1:])
    for (h, _, _), p, tm in zip(routed, pos, tms):
        xs = _dispatch(p, pad_rows, h, xs, tm=tm)
    ys = _expert_ffn(te, xs, *experts, layer=layer, n_tiles=n_tiles, tm=tile)
    return [_combine(p, x, gates, g_final, ys, tm=tm, final_norm=final_norm)
            for x, (_, _, gates), p, tm in zip(streams, routed, pos, tms)]


def _rope_tables(pos, rows):
    half = QK_ROPE // 2
    freqs = 1.0 / (ROPE_BASE ** (jnp.arange(half, dtype=F32) * (2.0 / QK_ROPE)))
    ang = pos.astype(F32)[:, None] * freqs[None, :]
    cos, sin = jnp.cos(ang), jnp.sin(ang)
    cos = jnp.tile(jnp.concatenate([cos, cos], axis=-1), (rows // pos.shape[0], N_HEADS))
    sin = jnp.tile(jnp.concatenate([-sin, sin], axis=-1), (rows // pos.shape[0], N_HEADS))
    return cos, sin


def _mla_weights(w_in, g_q, g_kv, w_uq, w_uk, w_uv, w_o):
    half = QK_ROPE // 2
    swap = jnp.concatenate([jnp.arange(half, QK_ROPE), jnp.arange(half)])
    c2 = Q_LORA + KV_LORA
    w_in_ext = jnp.concatenate([w_in, w_in[:, c2:][:, swap]], axis=1)
    uq = w_uq.reshape(Q_LORA, N_HEADS, QK_NOPE + QK_ROPE)
    rope_cols = uq[:, :, QK_NOPE:]
    a = jnp.transpose(w_uk, (1, 2, 0)).reshape(N_HEADS // 4, 4, QK_NOPE, KV_LORA)
    w_uk4 = jnp.einsum("pjnc,jk->pjnkc", a, jnp.eye(4, dtype=a.dtype)).reshape(N_HEADS // 4, 4 * QK_NOPE, 4 * KV_LORA)
    side = jax.nn.one_hot(jnp.arange(N_HEADS) % 2, 2, dtype=w_uv.dtype)
    w_uv2 = jnp.einsum("chv,hk->hckv", w_uv, side).reshape(N_HEADS, KV_LORA, 2 * V_HEAD)
    return {
        "w_in": w_in_ext.astype(BF16), "g_q": g_q.reshape(1, -1), "g_kv": g_kv.reshape(1, -1),
        "w_qn": uq[:, :, :QK_NOPE].reshape(Q_LORA, -1).astype(BF16),
        "w_qa": rope_cols.reshape(Q_LORA, -1).astype(BF16),
        "w_qb": rope_cols[:, :, swap].reshape(Q_LORA, -1).astype(BF16),
        "w_uk4": w_uk4.astype(BF16), "w_uv2": w_uv2.astype(BF16), "w_o": w_o.astype(BF16),
    }


def _router_weights(w_router, b_router):
    d = w_router.shape[0]
    wr = jnp.zeros((d, LANES), F32).at[:, :N_EXPERTS].set(w_router)
    hi = wr.astype(BF16)
    return {
        "r_hi": hi, "r_lo": (wr - hi.astype(F32)).astype(BF16),
        "r_b": jnp.zeros((1, LANES), F32).at[0, :N_EXPERTS].set(b_router),
    }


def _tile(n, want):
    if n <= want:
        return n
    t = want - want % 16
    while n % t:
        t -= 16
    return t


def kernel(x_prompt, x_sample, cache_ckv, cache_kpe, state_pool, page_table, meta_tokens, g_mix, g_ffn, g_final, pool_w, pool_scale, mla_w_in, mla_g_q, mla_g_kv, mla_w_uq, mla_w_uk, mla_w_uv, mla_w_o, ffn_w_gate, ffn_w_up, ffn_w_down, moe_w_router, moe_b_router, moe_w_gate, moe_w_up, moe_w_down):
    b, seq, d = x_prompt.shape
    db, s_len, _ = x_sample.shape
    depth = g_mix.shape[0]
    assert depth % 2 == 0
    past_len = page_table.shape[1] * cache_ckv.shape[2]
    n_main, n_samp = b * seq, db * s_len

    xm = x_prompt
    xs = x_sample
    xe = meta_tokens[None].astype(F32)

    tm_main = _tile(n_main, 1024)
    tm_samp = _tile(n_samp, 512)
    cos_m, sin_m = _rope_tables(N_META + jnp.arange(seq), max(seq, tm_main))
    cos_s, sin_s = _rope_tables(past_len + jnp.arange(s_len), tm_samp)
    cos_e, sin_e = _rope_tables(jnp.arange(N_META), N_META)

    experts = (moe_w_gate.astype(BF16), moe_w_up.astype(BF16), moe_w_down.astype(BF16))
    outs = {k: [] for k in ("ckv_p", "kpe_p", "pool_p", "ckv_s", "kpe_s", "pool_s")}
    for i in range(depth):
        j = i // 2
        last = i == depth - 1
        if i % 2 == 0:
            w_bf = pool_w[j].astype(BF16)
            zeros_e = jnp.zeros((1, HIST, d), F32)
            xe, tail_e = _pool_layer(xe, zeros_e, g_mix[i], w_bf, pool_scale[j], has_hist=False, tb=1, tl=N_META)
            xm, tail_m = _pool_layer(xm, tail_e, g_mix[i], w_bf, pool_scale[j], has_hist=True, tb=1,
                                     tl=min(512, seq))
            prev_s = jnp.concatenate([jnp.zeros((db, 1, d), F32), state_pool[j]], axis=1)
            xs, tail_s = _pool_layer(xs, prev_s, g_mix[i], w_bf, pool_scale[j], has_hist=True,
                                     tb=_tile(db, 64), tl=s_len)
            outs["pool_p"].append(tail_m[:, 1:])
            outs["pool_s"].append(tail_s[:, 1:])

            wg, wu, wd = ffn_w_gate[j].astype(BF16), ffn_w_up[j].astype(BF16), ffn_w_down[j].astype(BF16)
            xm = _ffn_dense(xm.reshape(n_main, d), g_ffn[i], wg, wu, wd, tm=tm_main).reshape(b, seq, d)
            small = jnp.concatenate([xs.reshape(n_samp, d), xe[0]], axis=0)
            small = _ffn_dense(small, g_ffn[i], wg, wu, wd, tm=small.shape[0])
            xs, xe = small[:n_samp].reshape(db, s_len, d), small[n_samp:][None]
        else:
            w = _mla_weights(mla_w_in[j], mla_g_q[j], mla_g_kv[j], mla_w_uq[j], mla_w_uk[j], mla_w_uv[j], mla_w_o[j])
            xe2, xm2, xs2 = xe[0], xm.reshape(n_main, d), xs.reshape(n_samp, d)
            ql_e, qp_e, ckv_e, kpe_e, ckvb_e, kpeb_e = _mla_proj(xe2, cos_e, sin_e, g_mix[i], w, tm=N_META, q_dtype=BF16)
            ql_m, qp_m, ckv_m, kpe_m, ckvb_m, kpeb_m = _mla_proj(xm2, cos_m, sin_m, g_mix[i], w, tm=tm_main, q_dtype=BF16)
            ql_s, qp_s, ckv_s, kpe_s, _, _ = _mla_proj(xs2, cos_s, sin_s, g_mix[i], w, tm=tm_samp, q_dtype=F32)

            o_e = _attn_causal(ql_e, qp_e, ckvb_e, kpeb_e, None, batch=1, tqb=N_META, tqs=N_META, tk=LANES)
            o_m = _attn_causal(ql_m, qp_m, ckvb_m, kpeb_m, (ckvb_e, kpeb_e), batch=b, tqb=min(256, seq), tqs=64,
                               tk=512)
            o_s = _attn_decode(ql_s, qp_s, ckv_s, kpe_s, cache_ckv, cache_kpe, page_table, layer=j, s_len=s_len)

            xe2 = _attn_out(o_e, xe2, w["w_uv2"], w["w_o"], tm=N_META)
            xm2 = _attn_out(o_m, xm2, w["w_uv2"], w["w_o"], tm=tm_main)
            xs2 = _attn_out(o_s, xs2, w["w_uv2"], w["w_o"], tm=tm_samp)

            outs["ckv_p"].append((ckv_e, ckv_m))
            outs["kpe_p"].append((kpe_e, kpe_m))
            outs["ckv_s"].append(ckv_s.reshape(db, s_len, KV_LORA))
            outs["kpe_s"].append(kpe_s.reshape(db, s_len, QK_ROPE))

            mw = _router_weights(moe_w_router[j], moe_b_router[j])
            small = jnp.concatenate([xs2, xe2], axis=0)
            xm, small = _moe([xm2, small], [_tile(n_main, 512), small.shape[0]], g_ffn[i], mw, experts, g_final,
                             layer=j, tile=512, final_norm=last)
            xm = xm.reshape(b, seq, d)
            xs, xe = small[:n_samp].reshape(db, s_len, d), small[n_samp:][None]

    def with_meta(pairs):
        n_l, width = len(pairs), pairs[0][0].shape[-1]
        meta = jnp.broadcast_to(jnp.stack([e for e, _ in pairs])[:, None], (n_l, b, N_META, width))
        main = jnp.stack([m for _, m in pairs]).reshape(n_l, b, seq, width)
        return jnp.concatenate([meta, main], axis=2)

    return (xm, xs, with_meta(outs["ckv_p"]), with_meta(outs["kpe_p"]), jnp.stack(outs["pool_p"]),
            jnp.stack(outs["ckv_s"]), jnp.stack(outs["kpe_s"]), jnp.stack(outs["pool_s"]))
```

```python
import functools

import jax
import jax.numpy as jnp
from jax import lax
from jax.experimental import pallas as pl
from jax.experimental.pallas import tpu as pltpu

EPS = 1e-6
N_META = 16
POOL_WINDOWS = (2, 4, 8, 16)
N_HEADS = 16
Q_LORA = 384
KV_LORA = 256
QK_NOPE = 64
QK_ROPE = 32
V_HEAD = 64
ROPE_BASE = 10000.0
ATTN_SCALE = (QK_NOPE + QK_ROPE) ** -0.5
Q_SCALE = ATTN_SCALE * 1.4426950408889634
N_EXPERTS = 8
TOP_K = 2

HIST = 16
LANES = 128
NEG = -1e30
VMEM_LIMIT = 56 * 1024 * 1024
BF16 = jnp.bfloat16
F32 = jnp.float32


def _dot(a, b):
    return jnp.dot(a, b, preferred_element_type=F32)


def _dot_nt(a, b):
    return lax.dot_general(a, b, (((1,), (1,)), ((), ())), preferred_element_type=F32)


def _rms(x, g):
    return x * lax.rsqrt(jnp.mean(x * x, axis=-1, keepdims=True) + EPS) * g


def _params(sem, vmem=VMEM_LIMIT):
    return pltpu.CompilerParams(dimension_semantics=sem, vmem_limit_bytes=vmem)


def _const_spec(shape):
    nd = len(shape)
    return pl.BlockSpec(shape, lambda *_: (0,) * nd)


def _pool_kernel(*refs, has_hist, multi_tile, tb, tl):
    if multi_tile:
        x_ref, halo_ref, prev_ref, g_ref, w_ref, sc_ref, o_ref, tail_ref, hext_ref = refs
    else:
        x_ref, prev_ref, g_ref, w_ref, sc_ref, o_ref, tail_ref, hext_ref = refs
        halo_ref = None
    t = pl.program_id(1)
    d = x_ref.shape[-1]
    gc = d // len(POOL_WINDOWS)
    g = g_ref[...]
    x = x_ref[...]
    h = _rms(x, g)
    hext_ref[:, HIST:, :] = h
    prev = jnp.broadcast_to(prev_ref[...], (tb, HIST, d))
    if multi_tile:
        @pl.when(t == 0)
        def _():
            hext_ref[:, :HIST, :] = prev

        @pl.when(t > 0)
        def _():
            hext_ref[:, :HIST, :] = _rms(halo_ref[...], g)
    else:
        hext_ref[:, :HIST, :] = prev

    for gi, w in enumerate(POOL_WINDOWS):
        c0, c1 = gi * gc, (gi + 1) * gc
        win = hext_ref[:, pl.ds(HIST, tl), c0:c1]
        for i in range(1, w):
            win = win + hext_ref[:, pl.ds(HIST - i, tl), c0:c1]
        hg = hext_ref[:, pl.ds(HIST, tl), c0:c1]
        if has_hist:
            dg = win * (1.0 / w) - hg
        else:
            row = lax.broadcasted_iota(jnp.int32, (tb, tl, gc), 1) + t * tl
            cnt = jnp.minimum(row + 1, w).astype(F32)
            dg = win / cnt - hg
        dg = dg.reshape(tb * tl, gc).astype(BF16)
        yg = _dot(dg, w_ref[gi]).reshape(tb, tl, gc)
        o_ref[:, :, c0:c1] = x_ref[:, :, c0:c1] + yg * sc_ref[:, c0:c1]
    tail_ref[...] = hext_ref[:, pl.ds(tl, HIST), :]


def _pool_layer(x, prev, g, w_bf, scale, *, has_hist, tb, tl):
    b, l, d = x.shape
    n_t = l // tl
    multi = n_t > 1
    assert l % tl == 0 and b % tb == 0 and (not multi or (tb == 1 and tl % HIST == 0))
    shared = prev.shape[0] == 1
    per_tile = tl // HIST
    in_specs = [pl.BlockSpec((tb, tl, d), lambda i, t: (i, t, 0))]
    args = [x]
    if multi:
        in_specs.append(pl.BlockSpec((tb, HIST, d), lambda i, t: (i, jnp.maximum(t * per_tile - 1, 0), 0)))
        args.append(x)
    in_specs += [
        pl.BlockSpec((1 if shared else tb, HIST, d), (lambda i, t: (0, 0, 0)) if shared else (lambda i, t: (i, 0, 0))),
        _const_spec((1, d)), _const_spec(w_bf.shape), _const_spec((1, d)),
    ]
    args += [prev, g.reshape(1, d), w_bf, scale.reshape(1, d)]
    return pl.pallas_call(
        functools.partial(_pool_kernel, has_hist=has_hist, multi_tile=multi, tb=tb, tl=tl),
        grid=(b // tb, n_t),
        in_specs=in_specs,
        out_specs=[pl.BlockSpec((tb, tl, d), lambda i, t: (i, t, 0)),
                   pl.BlockSpec((tb, HIST, d), lambda i, t: (i, 0, 0))],
        out_shape=[jax.ShapeDtypeStruct((b, l, d), F32), jax.ShapeDtypeStruct((b, HIST, d), F32)],
        scratch_shapes=[pltpu.VMEM((tb, tl + HIST, d), F32)],
        compiler_params=_params(("parallel", "arbitrary")),
        name="pool_mixer",
    )(*args)


FF_CHUNK = 256


def _swiglu_into(acc_ref, h, wg_ref, wu_ref, wd_ref, lead=()):
    n_ff = wg_ref.shape[-1]
    for c in range(0, n_ff, FF_CHUNK):
        gg = _dot(h, wg_ref[lead + (slice(None), slice(c, c + FF_CHUNK))])
        uu = _dot(h, wu_ref[lead + (slice(None), slice(c, c + FF_CHUNK))])
        a = (gg * jax.nn.sigmoid(gg) * uu).astype(BF16)
        y = _dot(a, wd_ref[lead + (slice(c, c + FF_CHUNK), slice(None))])
        if c == 0:
            acc_ref[...] = y
        else:
            acc_ref[...] += y


def _ffn_kernel(x_ref, g_ref, wg_ref, wu_ref, wd_ref, o_ref, acc_ref):
    x = x_ref[...]
    h = _rms(x, g_ref[...]).astype(BF16)
    _swiglu_into(acc_ref, h, wg_ref, wu_ref, wd_ref)
    o_ref[...] = x + acc_ref[...]


def _ffn_dense(x, g, wg, wu, wd, *, tm):
    n, d = x.shape
    assert n % tm == 0
    resident = lambda shape: pl.BlockSpec(shape, lambda i: (0, 0), pipeline_mode=pl.Buffered(1))
    return pl.pallas_call(
        _ffn_kernel,
        grid=(n // tm,),
        in_specs=[pl.BlockSpec((tm, d), lambda i: (i, 0)), _const_spec((1, d)),
                  resident(wg.shape), resident(wu.shape), resident(wd.shape)],
        out_specs=pl.BlockSpec((tm, d), lambda i: (i, 0)),
        out_shape=jax.ShapeDtypeStruct((n, d), F32),
        scratch_shapes=[pltpu.VMEM((tm, d), F32)],
        compiler_params=_params(("parallel",)),
        name="ffn_dense",
    )(x, g.reshape(1, d), wg, wu, wd)


def _mla_proj_kernel(x_ref, cos_ref, sin_ref, g_ref, win_ref, gq_ref, gkv_ref, wqn_ref, wqa_ref, wqb_ref, wuk_ref,
                     qlat_ref, qpe_ref, ckv_ref, kpe_ref, ckvb_ref, kpeb_ref):
    h = _rms(x_ref[...], g_ref[...]).astype(BF16)
    proj = _dot(h, win_ref[...])
    c1, c2, c3 = Q_LORA, Q_LORA + KV_LORA, Q_LORA + KV_LORA + QK_ROPE
    cqn = _rms(proj[:, :c1], gq_ref[...]).astype(BF16)
    ckv = _rms(proj[:, c1:c2], gkv_ref[...])
    cos = cos_ref[...]
    sin = sin_ref[...]
    kpe = proj[:, c2:c3] * cos[:, :QK_ROPE] + proj[:, c3:c3 + QK_ROPE] * sin[:, :QK_ROPE]
    ckv_ref[...] = ckv
    kpe_ref[...] = kpe
    ckvb_ref[...] = ckv.astype(BF16)
    kpeb_ref[...] = kpe.astype(BF16)

    qn = _dot(cqn, wqn_ref[...]).astype(BF16)
    quad = 4 * QK_NOPE
    for p in range(N_HEADS // 4):
        ql = _dot(qn[:, p * quad:(p + 1) * quad], wuk_ref[p]) * Q_SCALE
        for j in range(4):
            qlat_ref[4 * p + j] = ql[:, j * KV_LORA:(j + 1) * KV_LORA].astype(qlat_ref.dtype)
    qr = (_dot(cqn, wqa_ref[...]) * cos + _dot(cqn, wqb_ref[...]) * sin) * Q_SCALE
    for hd in range(N_HEADS):
        qpe_ref[hd] = qr[:, hd * QK_ROPE:(hd + 1) * QK_ROPE].astype(qpe_ref.dtype)


def _mla_proj(x, cos, sin, g, w, *, tm, q_dtype):
    n, d = x.shape
    assert n % tm == 0 and cos.shape[0] % tm == 0
    n_tab = cos.shape[0] // tm
    hr = N_HEADS * QK_ROPE
    tok = lambda width: pl.BlockSpec((tm, width), lambda i: (i, 0))
    tab = pl.BlockSpec((tm, hr), lambda i: (i % n_tab, 0))
    headed = lambda width: pl.BlockSpec((N_HEADS, tm, width), lambda i: (0, i, 0))
    return pl.pallas_call(
        _mla_proj_kernel,
        grid=(n // tm,),
        in_specs=[tok(d), tab, tab, _const_spec((1, d)), _const_spec(w["w_in"].shape),
                  _const_spec((1, Q_LORA)), _const_spec((1, KV_LORA)), _const_spec(w["w_qn"].shape),
                  _const_spec(w["w_qa"].shape), _const_spec(w["w_qb"].shape), _const_spec(w["w_uk4"].shape)],
        out_specs=[headed(KV_LORA), headed(QK_ROPE), tok(KV_LORA), tok(QK_ROPE), tok(KV_LORA), tok(QK_ROPE)],
        out_shape=[jax.ShapeDtypeStruct((N_HEADS, n, KV_LORA), q_dtype),
                   jax.ShapeDtypeStruct((N_HEADS, n, QK_ROPE), q_dtype),
                   jax.ShapeDtypeStruct((n, KV_LORA), F32), jax.ShapeDtypeStruct((n, QK_ROPE), F32),
                   jax.ShapeDtypeStruct((n, KV_LORA), BF16), jax.ShapeDtypeStruct((n, QK_ROPE), BF16)],
        compiler_params=_params(("parallel",)),
        name="mla_proj",
    )(x, cos, sin, g.reshape(1, d), w["w_in"], w["g_q"], w["g_kv"], w["w_qn"], w["w_qa"], w["w_qb"], w["w_uk4"])


ATTN_SPLIT = 2
ATTN_EDGE = 256
ROW_CHUNK = 32


def _attn_kernel(*refs, n_pre, t_len, tqb, tqs, tk):
    if n_pre:
        q_ref, qr_ref, k_ref, kr_ref, pk_ref, pkr_ref, o_ref, kf, krf, m_ref, l_ref, a_ref, acc_ref, s_ref, p_ref = refs
    else:
        q_ref, qr_ref, k_ref, kr_ref, o_ref, kf, krf, m_ref, l_ref, a_ref, acc_ref, s_ref, p_ref = refs
    qi = pl.program_id(1)
    n_keys = n_pre + t_len
    hg = N_HEADS // ATTN_SPLIT
    rows = hg * tqs
    tk_shift = tk.bit_length() - 1
    te = min(tk, ATTN_EDGE)
    te_shift = te.bit_length() - 1

    @pl.when(qi == 0)
    def _():
        if n_pre:
            kf[0:n_pre, :] = pk_ref[...]
            krf[0:n_pre, :] = pkr_ref[...]
        kf[n_pre:n_keys, :] = k_ref[...]
        krf[n_pre:n_keys, :] = kr_ref[...]
        if kf.shape[0] > n_keys:
            kf[n_keys:, :] = jnp.zeros((kf.shape[0] - n_keys, KV_LORA), kf.dtype)
            krf[n_keys:, :] = jnp.zeros((kf.shape[0] - n_keys, QK_ROPE), krf.dtype)

    def sub_block(i, carry):
        t0 = pl.multiple_of(i * tqs, tqs)
        p0 = n_pre + qi * tqb + i * tqs
        for g in range(ATTN_SPLIT):
            m_ref[g] = jnp.full((rows, LANES), NEG, F32)
            l_ref[g] = jnp.zeros((rows, LANES), F32)
            acc_ref[g] = jnp.zeros((rows, KV_LORA), F32)

        def update(k0, width, masked):
            k = kf[pl.ds(k0, width), :]
            kr = krf[pl.ds(k0, width), :]
            n_rep = width // LANES
            chunks = [slice(c, c + ROW_CHUNK) for c in range(0, rows, ROW_CHUNK)]
            for g in range(ATTN_SPLIT):
                q = q_ref[g * hg:(g + 1) * hg, pl.ds(t0, tqs), :].reshape(rows, KV_LORA)
                qr = qr_ref[g * hg:(g + 1) * hg, pl.ds(t0, tqs), :].reshape(rows, QK_ROPE)
                s = _dot_nt(q, k) + _dot_nt(qr, kr)
                if masked:
                    r = lax.broadcasted_iota(jnp.int32, s.shape, 0) & (tqs - 1)
                    c = lax.broadcasted_iota(jnp.int32, s.shape, 1)
                    s = jnp.where(c - r <= p0 - k0, s, NEG)
                s_ref[g, :, :width] = s
            for g in range(ATTN_SPLIT):
                for rs in chunks:
                    m_old = m_ref[g, rs, :]
                    row_max = jnp.max(s_ref[g, rs, :width], axis=-1, keepdims=True)
                    m_new = jnp.maximum(m_old, jnp.broadcast_to(row_max, (ROW_CHUNK, LANES)))
                    a_ref[g, rs, :] = jnp.exp2(m_old - m_new)
                    m_ref[g, rs, :] = m_new
                for rs in chunks:
                    p = jnp.exp2(s_ref[g, rs, :width] - jnp.concatenate([m_ref[g, rs, :]] * n_rep, axis=-1))
                    part = p[:, :LANES]
                    for c in range(1, n_rep):
                        part = part + p[:, c * LANES:(c + 1) * LANES]
                    l_ref[g, rs, :] = a_ref[g, rs, :] * l_ref[g, rs, :] + part
                    p_ref[g, rs, :width] = p.astype(BF16)
            for g in range(ATTN_SPLIT):
                alpha = jnp.concatenate([a_ref[g]] * (KV_LORA // LANES), axis=-1)
                acc_ref[g] = alpha * acc_ref[g] + _dot(p_ref[g, :, :width], k)

        def full_tile(j, c):
            update(pl.multiple_of(j * tk, tk), tk, False)
            return c

        def edge_tile(j, c):
            update(pl.multiple_of(j * te, te), te, True)
            return c

        n_full = (p0 + 1) >> tk_shift
        lax.fori_loop(0, n_full, full_tile, 0)
        lax.fori_loop(n_full * (tk // te), ((p0 + tqs - 1) >> te_shift) + 1, edge_tile, 0)
        for g in range(ATTN_SPLIT):
            o = acc_ref[g] / jnp.sum(l_ref[g], axis=-1, keepdims=True)
            o_ref[g * hg:(g + 1) * hg, pl.ds(t0, tqs), :] = o.reshape(hg, tqs, KV_LORA).astype(o_ref.dtype)
        return carry

    lax.fori_loop(0, tqb // tqs, sub_block, 0)


def _attn_causal(q_lat, q_pe, ckv_b, kpe_b, prefix, *, batch, tqb, tqs, tk):
    n = ckv_b.shape[0]
    t_len = n // batch
    n_pre = 0 if prefix is None else prefix[0].shape[0]
    assert t_len % tqb == 0 and tqb % tqs == 0 and tk % tqs == 0 and tk & (tk - 1) == 0 and tqs & (tqs - 1) == 0
    assert tqs % 16 == 0 and n_pre % 16 == 0 and (n_pre + t_len) % 16 == 0
    n_q = t_len // tqb
    key_rows = -(-(n_pre + t_len) // tk) * tk
    qmap = lambda b, qi: (0, b * n_q + qi, 0)
    kmap = lambda b, qi: (b, 0)
    in_specs = [pl.BlockSpec((N_HEADS, tqb, KV_LORA), qmap), pl.BlockSpec((N_HEADS, tqb, QK_ROPE), qmap),
                pl.BlockSpec((t_len, KV_LORA), kmap), pl.BlockSpec((t_len, QK_ROPE), kmap)]
    args = [q_lat, q_pe, ckv_b, kpe_b]
    if n_pre:
        in_specs += [_const_spec(prefix[0].shape), _const_spec(prefix[1].shape)]
        args += list(prefix)
    rows = N_HEADS // ATTN_SPLIT * tqs
    return pl.pallas_call(
        functools.partial(_attn_kernel, n_pre=n_pre, t_len=t_len, tqb=tqb, tqs=tqs, tk=tk),
        grid=(batch, n_q),
        in_specs=in_specs,
        out_specs=pl.BlockSpec((N_HEADS, tqb, KV_LORA), qmap),
        out_shape=jax.ShapeDtypeStruct((N_HEADS, n, KV_LORA), q_lat.dtype),
        scratch_shapes=[pltpu.VMEM((key_rows, KV_LORA), BF16), pltpu.VMEM((key_rows, QK_ROPE), BF16),
                        pltpu.VMEM((ATTN_SPLIT, rows, LANES), F32), pltpu.VMEM((ATTN_SPLIT, rows, LANES), F32),
                        pltpu.VMEM((ATTN_SPLIT, rows, LANES), F32), pltpu.VMEM((ATTN_SPLIT, rows, KV_LORA), F32),
                        pltpu.VMEM((ATTN_SPLIT, rows, tk), F32), pltpu.VMEM((ATTN_SPLIT, rows, tk), BF16)],
        compiler_params=_params(("parallel", "arbitrary")),
        name="attn_causal",
    )(*args)


KEY_CHUNK = 2048


def _attn_decode_kernel(pt_ref, q_ref, qr_ref, ks_ref, krs_ref, cckv_ref, ckpe_ref, o_ref,
                        kbuf, rbuf, sem, *, layer, n_pages, page, s_len):
    b = pl.program_id(0)
    nb = pl.num_programs(0)
    slot = b % 2
    past = n_pages * page

    def copies(bb, sl, p):
        pg = pt_ref[bb, p]
        off = pl.multiple_of(p * page, page)
        return (pltpu.make_async_copy(cckv_ref.at[layer, pg], kbuf.at[sl, pl.ds(off, page)], sem.at[0, sl]),
                pltpu.make_async_copy(ckpe_ref.at[layer, pg], rbuf.at[sl, :, pl.ds(off, page)], sem.at[1, sl]))

    def fetch(bb, sl):
        def body(p, c):
            for cp in copies(bb, sl, p):
                cp.start()
            return c
        lax.fori_loop(0, n_pages, body, 0, unroll=8)

    @pl.when(b == 0)
    def _():
        kbuf[:, past:, :] = jnp.zeros((2, LANES, KV_LORA), F32)
        fetch(0, 0)

    @pl.when(b + 1 < nb)
    def _():
        fetch(b + 1, 1 - slot)

    pltpu.make_async_copy(kbuf.at[slot, pl.ds(0, past)], kbuf.at[slot, pl.ds(0, past)], sem.at[0, slot]).wait()
    pltpu.make_async_copy(rbuf.at[slot, :, pl.ds(0, past)], rbuf.at[slot, :, pl.ds(0, past)], sem.at[1, slot]).wait()

    kbuf[slot, past:past + s_len, :] = ks_ref[...]
    rbuf[slot, :, past:] = krs_ref[0]

    rows = N_HEADS * s_len
    q = q_ref[...].reshape(rows, KV_LORA).astype(BF16)
    qr = qr_ref[...].reshape(rows, QK_ROPE).astype(BF16)
    bounds = list(range(0, past, KEY_CHUNK)) + [past]
    sizes = [min(KEY_CHUNK, past - c) for c in bounds[:-1]] + [LANES]
    ks, ss = [], []
    for c0, sz in zip(bounds, sizes):
        k = kbuf[slot, c0:c0 + sz, :].astype(BF16)
        kr_t = rbuf[slot, :, c0:c0 + sz].astype(BF16)
        s = _dot_nt(q, k) + _dot(qr, kr_t)
        if c0 == past:
            r = lax.broadcasted_iota(jnp.int32, s.shape, 0) & (s_len - 1)
            c = lax.broadcasted_iota(jnp.int32, s.shape, 1)
            s = jnp.where(c <= r, s, NEG)
        ks.append(k)
        ss.append(s)
    m = functools.reduce(jnp.maximum, [jnp.max(s, axis=-1, keepdims=True) for s in ss])
    l = jnp.zeros((rows, 1), F32)
    acc = jnp.zeros((rows, KV_LORA), F32)
    for k, s in zip(ks, ss):
        p = jnp.exp2(s - m)
        l = l + jnp.sum(p, axis=-1, keepdims=True)
        acc = acc + _dot(p.astype(BF16), k)
    o_ref[...] = (acc / l).reshape(N_HEADS, s_len, KV_LORA)


def _attn_decode(q_lat, q_pe, ckv_s, kpe_s, cache_ckv, cache_kpe, page_table, *, layer, s_len):
    db, n_pages = page_table.shape
    page = cache_ckv.shape[2]
    assert s_len & (s_len - 1) == 0 and s_len <= LANES and page % LANES == 0
    buf_rows = n_pages * page + LANES
    cache_kpe_t = jnp.swapaxes(cache_kpe, 2, 3)
    kpe_new_t = jnp.pad(jnp.swapaxes(kpe_s.reshape(db, s_len, QK_ROPE), 1, 2), ((0, 0), (0, 0), (0, LANES - s_len)))
    qmap = lambda b, pt: (0, b, 0)
    grid_spec = pltpu.PrefetchScalarGridSpec(
        num_scalar_prefetch=1,
        grid=(db,),
        in_specs=[pl.BlockSpec((N_HEADS, s_len, KV_LORA), qmap), pl.BlockSpec((N_HEADS, s_len, QK_ROPE), qmap),
                  pl.BlockSpec((s_len, KV_LORA), lambda b, pt: (b, 0)),
                  pl.BlockSpec((1, QK_ROPE, LANES), lambda b, pt: (b, 0, 0)),
                  pl.BlockSpec(memory_space=pl.ANY), pl.BlockSpec(memory_space=pl.ANY)],
        out_specs=pl.BlockSpec((N_HEADS, s_len, KV_LORA), qmap),
        scratch_shapes=[pltpu.VMEM((2, buf_rows, KV_LORA), F32), pltpu.VMEM((2, QK_ROPE, buf_rows), F32),
                        pltpu.SemaphoreType.DMA((2, 2))],
    )
    return pl.pallas_call(
        functools.partial(_attn_decode_kernel, layer=layer, n_pages=n_pages, page=page, s_len=s_len),
        grid_spec=grid_spec,
        out_shape=jax.ShapeDtypeStruct((N_HEADS, db * s_len, KV_LORA), F32),
        compiler_params=_params(("arbitrary",)),
        name="attn_decode",
    )(page_table, q_lat, q_pe, ckv_s, kpe_new_t, cache_ckv, cache_kpe_t)


def _attn_out_kernel(o_ref, x_ref, wuv_ref, wo_ref, out_ref):
    pieces = []
    for p in range(N_HEADS // 2):
        pieces.append(_dot(o_ref[2 * p].astype(BF16), wuv_ref[2 * p])
                      + _dot(o_ref[2 * p + 1].astype(BF16), wuv_ref[2 * p + 1]))
    o = jnp.concatenate(pieces, axis=-1).astype(BF16)
    out_ref[...] = x_ref[...] + _dot(o, wo_ref[...])


def _attn_out(o_lat, x, w_uv2, w_o, *, tm):
    n, d = x.shape
    assert n % tm == 0
    return pl.pallas_call(
        _attn_out_kernel,
        grid=(n // tm,),
        in_specs=[pl.BlockSpec((N_HEADS, tm, KV_LORA), lambda i: (0, i, 0)), pl.BlockSpec((tm, d), lambda i: (i, 0)),
                  _const_spec(w_uv2.shape), _const_spec(w_o.shape)],
        out_specs=pl.BlockSpec((tm, d), lambda i: (i, 0)),
        out_shape=jax.ShapeDtypeStruct((n, d), F32),
        compiler_params=_params(("parallel",)),
        name="attn_out",
    )(o_lat, x, w_uv2, w_o)


SUB = 8


def _load_rows(ref, n):
    return jnp.concatenate([ref[pl.ds(c, n, stride=SUB), :] for c in range(SUB)], axis=-1)


def _store_rows(ref, x):
    n = x.shape[0]
    for c in range(SUB):
        ref[pl.ds(c, n, stride=SUB), :] = x[:, c * LANES:(c + 1) * LANES]


def _router_kernel(x_ref, g_ref, whi_ref, wlo_ref, b_ref, h_ref, idx_ref, gate_ref):
    h = _rms(x_ref[...], g_ref[...])
    _store_rows(h_ref, h)
    h_hi = h.astype(BF16)
    h_lo = (h - h_hi.astype(F32)).astype(BF16)
    logits = _dot(h_hi, whi_ref[...]) + _dot(h_lo, whi_ref[...]) + _dot(h_hi, wlo_ref[...])
    lane = lax.broadcasted_iota(jnp.int32, logits.shape, 1)
    lane_f = lane.astype(F32)
    sel = jnp.where(lane < N_EXPERTS, logits + b_ref[...], NEG)
    m1 = jnp.max(sel, axis=-1, keepdims=True)
    i1 = jnp.min(jnp.where(sel == m1, lane_f, float(LANES)), axis=-1, keepdims=True)
    sel2 = jnp.where(lane_f == i1, NEG, sel)
    m2 = jnp.max(sel2, axis=-1, keepdims=True)
    i2 = jnp.min(jnp.where(sel2 == m2, lane_f, float(LANES)), axis=-1, keepdims=True)
    l1 = jnp.sum(jnp.where(lane_f == i1, logits, 0.0), axis=-1, keepdims=True)
    l2 = jnp.sum(jnp.where(lane_f == i2, logits, 0.0), axis=-1, keepdims=True)
    mx = jnp.maximum(l1, l2)
    e1 = jnp.exp(l1 - mx)
    e2 = jnp.exp(l2 - mx)
    den = e1 + e2
    idx_ref[...] = jnp.where(lane == 0, i1, jnp.where(lane == 1, i2, 0.0)).astype(jnp.int32)
    gate_ref[...] = jnp.where(lane == 0, e1 / den, jnp.where(lane == 1, e2 / den, 0.0))


def _router(x, g, w_hi, w_lo, bias, *, tm):
    n, d = x.shape
    assert n % tm == 0 and d == SUB * LANES
    tok = lambda width: pl.BlockSpec((tm, width), lambda i: (i, 0))
    return pl.pallas_call(
        _router_kernel,
        grid=(n // tm,),
        in_specs=[tok(d), _const_spec((1, d)), _const_spec(w_hi.shape), _const_spec(w_lo.shape),
                  _const_spec((1, LANES))],
        out_specs=[pl.BlockSpec((tm * SUB, LANES), lambda i: (i, 0)), tok(LANES), tok(LANES)],
        out_shape=[jax.ShapeDtypeStruct((n * SUB, LANES), F32), jax.ShapeDtypeStruct((n, LANES), jnp.int32),
                   jax.ShapeDtypeStruct((n, LANES), F32)],
        compiler_params=_params(("parallel",)),
        name="moe_router",
    )(x, g.reshape(1, d), w_hi, w_lo, bias)


def _dispatch_kernel(pos_ref, pad_ref, h_ref, *rest, tm, first):
    if first:
        out_ref, zero_ref, sem, fill_sem = rest
        n_pad = pad_ref.shape[0]

        @pl.when(pl.program_id(0) == 0)
        def _():
            zero_ref[...] = jnp.zeros(zero_ref.shape, F32)

            def fill(r, c):
                dst = pl.multiple_of(pad_ref[r] * SUB, SUB)
                pltpu.make_async_copy(zero_ref, out_ref.at[pl.ds(dst, SUB)], fill_sem).start()
                return c

            lax.fori_loop(0, n_pad, fill, 0, unroll=8)
            pltpu.make_async_copy(out_ref.at[pl.ds(0, n_pad * SUB)], out_ref.at[pl.ds(0, n_pad * SUB)], fill_sem).wait()
    else:
        _, out_ref, sem = rest
    base = pl.program_id(0) * tm

    def start(r, c):
        src = h_ref.at[pl.ds(pl.multiple_of(r * SUB, SUB), SUB)]
        for k in range(TOP_K):
            dst = pl.multiple_of(pos_ref[TOP_K * (base + r) + k] * SUB, SUB)
            pltpu.make_async_copy(src, out_ref.at[pl.ds(dst, SUB)], sem).start()
        return c

    lax.fori_loop(0, tm, start, 0, unroll=8)
    for k in range(TOP_K):
        pltpu.make_async_copy(h_ref, out_ref.at[pl.ds(0, tm * SUB)], sem).wait()


def _dispatch(pos, pad_rows, h, sorted_rows, *, tm):
    n = h.shape[0] // SUB
    assert n % tm == 0
    first = not hasattr(sorted_rows, "dtype")
    h_spec = pl.BlockSpec((tm * SUB, LANES), lambda i, pos, pad: (i, 0))
    any_spec = pl.BlockSpec(memory_space=pl.ANY)
    grid_spec = pltpu.PrefetchScalarGridSpec(
        num_scalar_prefetch=2,
        grid=(n // tm,),
        in_specs=[h_spec] if first else [h_spec, any_spec],
        out_specs=any_spec,
        scratch_shapes=([pltpu.VMEM((SUB, LANES), F32), pltpu.SemaphoreType.DMA(()), pltpu.SemaphoreType.DMA(())]
                        if first else [pltpu.SemaphoreType.DMA(())]),
    )
    shape = (sorted_rows * SUB, LANES) if first else sorted_rows.shape
    return pl.pallas_call(
        functools.partial(_dispatch_kernel, tm=tm, first=first),
        grid_spec=grid_spec,
        out_shape=jax.ShapeDtypeStruct(shape, F32),
        input_output_aliases={} if first else {3: 0},
        compiler_params=_params(("arbitrary",)),
        name="moe_dispatch",
    )(*((pos, pad_rows, h) if first else (pos, pad_rows, h, sorted_rows)))


def _expert_kernel(te_ref, x_ref, wg_ref, wu_ref, wd_ref, o_ref, acc_ref, *, n_tiles):
    t = pl.program_id(0)

    tm = acc_ref.shape[0]

    @pl.when(t < te_ref[n_tiles])
    def _():
        _swiglu_into(acc_ref, _load_rows(x_ref, tm).astype(BF16), wg_ref, wu_ref, wd_ref, lead=(0, 0))
        _store_rows(o_ref, acc_ref[...])

    @pl.when(t >= te_ref[n_tiles])
    def _():
        o_ref[...] = jnp.zeros(o_ref.shape, F32)


def _expert_ffn(te, xs, wg, wu, wd, *, layer, n_tiles, tm):
    wmap = lambda t, te: (layer, te[t], 0, 0)
    xmap = lambda t, te: (jnp.minimum(t, te[n_tiles] - 1), 0)
    grid_spec = pltpu.PrefetchScalarGridSpec(
        num_scalar_prefetch=1,
        grid=(n_tiles,),
        in_specs=[pl.BlockSpec((tm * SUB, LANES), xmap),
                  pl.BlockSpec((1, 1) + wg.shape[2:], wmap), pl.BlockSpec((1, 1) + wu.shape[2:], wmap),
                  pl.BlockSpec((1, 1) + wd.shape[2:], wmap)],
        out_specs=pl.BlockSpec((tm * SUB, LANES), lambda t, te: (t, 0)),
        scratch_shapes=[pltpu.VMEM((tm, SUB * LANES), F32)],
    )
    return pl.pallas_call(
        functools.partial(_expert_kernel, n_tiles=n_tiles),
        grid_spec=grid_spec,
        out_shape=jax.ShapeDtypeStruct((n_tiles * tm * SUB, LANES), F32),
        compiler_params=_params(("arbitrary",)),
        name="moe_experts",
    )(te, xs, wg, wu, wd)


def _combine_kernel(pos_ref, x_ref, gate_ref, gf_ref, y_ref, o_ref, buf, sem, *, tm, final_norm):
    i = pl.program_id(0)
    slot = i % 2

    def gather(step, sl):
        base = step * tm

        def start(r, c):
            row = pl.multiple_of(r * SUB, SUB)
            for k in range(TOP_K):
                src = pl.multiple_of(pos_ref[TOP_K * (base + r) + k] * SUB, SUB)
                pltpu.make_async_copy(y_ref.at[pl.ds(src, SUB)], buf.at[sl, k, pl.ds(row, SUB)], sem.at[sl]).start()
            return c

        lax.fori_loop(0, tm, start, 0, unroll=8)

    @pl.when(i == 0)
    def _():
        gather(0, 0)

    @pl.when(i + 1 < pl.num_programs(0))
    def _():
        gather(i + 1, 1 - slot)

    for k in range(TOP_K):
        pltpu.make_async_copy(y_ref.at[pl.ds(0, tm * SUB)], buf.at[slot, k], sem.at[slot]).wait()
    gate = gate_ref[...]
    out = (x_ref[...] + gate[:, 0:1] * _load_rows(buf.at[slot, 0], tm)
           + gate[:, 1:2] * _load_rows(buf.at[slot, 1], tm))
    if final_norm:
        out = _rms(out, gf_ref[...])
    o_ref[...] = out


def _combine(pos, x, gates, g_final, y, *, tm, final_norm):
    n, d = x.shape
    assert n % tm == 0
    grid_spec = pltpu.PrefetchScalarGridSpec(
        num_scalar_prefetch=1,
        grid=(n // tm,),
        in_specs=[pl.BlockSpec((tm, d), lambda i, pos: (i, 0)), pl.BlockSpec((tm, LANES), lambda i, pos: (i, 0)),
                  pl.BlockSpec((1, d), lambda i, pos: (0, 0)), pl.BlockSpec(memory_space=pl.ANY)],
        out_specs=pl.BlockSpec((tm, d), lambda i, pos: (i, 0)),
        scratch_shapes=[pltpu.VMEM((2, TOP_K, tm * SUB, LANES), F32), pltpu.SemaphoreType.DMA((2,))],
    )
    return pl.pallas_call(
        functools.partial(_combine_kernel, tm=tm, final_norm=final_norm),
        grid_spec=grid_spec,
        out_shape=jax.ShapeDtypeStruct((n, d), F32),
        compiler_params=_params(("arbitrary",)),
        name="moe_combine",
    )(pos, x, gates, g_final.reshape(1, d), y)


def _route_positions(idx, tm, n_tiles):
    e = idx[:, :TOP_K].reshape(-1)
    n_pairs = e.shape[0]
    n_chunks = -(-n_pairs // LANES)
    e_pad = jnp.pad(e, (0, n_chunks * LANES - n_pairs), constant_values=N_EXPERTS).reshape(n_chunks, LANES)
    onehot = (e_pad[:, :, None] == jnp.arange(N_EXPERTS, dtype=jnp.int32)[None, None, :]).astype(F32)
    within = jnp.einsum("ij,cjk->cik", jnp.tril(jnp.ones((LANES, LANES), F32)), onehot)
    chunk_total = within[:, -1, :]
    chunk_end = jnp.cumsum(chunk_total, axis=0)
    csum = within + (chunk_end - chunk_total)[:, None, :]
    rank = (jnp.sum(csum * onehot, axis=2).reshape(-1)[:n_pairs] - 1.0).astype(jnp.int32)
    counts = chunk_end[-1].astype(jnp.int32)
    padded = (counts + tm - 1) // tm * tm
    ends = jnp.cumsum(padded)
    pos = (ends - padded)[e] + rank
    n_used = ends[-1] // tm
    starts = jnp.arange(n_tiles, dtype=jnp.int32) * tm
    te = jnp.sum((starts[:, None] >= ends[None, :]).astype(jnp.int32), axis=1)
    te = jnp.minimum(te, jnp.take(te, jnp.maximum(n_used - 1, 0)))
    n_gap = n_tiles * tm - e.shape[0]
    gaps = jnp.concatenate([padded - counts, (n_tiles * tm - ends[-1])[None]])
    gap_start = jnp.concatenate([ends - padded + counts, ends[-1:]])
    gap_end = jnp.cumsum(gaps)
    j = jnp.arange(n_gap, dtype=jnp.int32)
    which = jnp.sum((j[:, None] >= gap_end[None, :]).astype(jnp.int32), axis=1)
    pad = jnp.take(gap_start, which) + j - jnp.take(gap_end - gaps, which)
    return (pos.astype(jnp.int32), pad.astype(jnp.int32), jnp.concatenate([te, n_used[None]]).astype(jnp.int32))


def _moe(streams, tms, g, w, experts, g_final, *, layer, tile, final_norm):
    routed = [_router(x, g, w["r_hi"], w["r_lo"], w["r_b"], tm=tm) for x, tm in zip(streams, tms)]
    n_all = sum(x.shape[0] for x in streams)
    n_tiles = -(-TOP_K * n_all // tile) + N_EXPERTS
    pos, pad_rows, te = _route_positions(jnp.concatenate([idx for _, idx, _ in routed], axis=0), tile, n_tiles)
    bounds = [0]
    for x in streams:
        bounds.append(bounds[-1] + TOP_K * x.shape[0])
    pos = [pos[a:b] for a, b in zip(bounds[:-1], bounds[1:])]
    xs = n_tiles * tile
    pad_rows = jnp.concatenate([pad_rows] + pos[1:])
    for (h, _, _), p, tm in zip(routed, pos, tms):
        xs = _dispatch(p, pad_rows, h, xs, tm=tm)
    ys = _expert_ffn(te, xs, *experts, layer=layer, n_tiles=n_tiles, tm=tile)
    return [_combine(p, x, gates, g_final, ys, tm=tm, final_norm=final_norm)
            for x, (_, _, gates), p, tm in zip(streams, routed, pos, tms)]


def _rope_tables(pos, rows):
    half = QK_ROPE // 2
    freqs = 1.0 / (ROPE_BASE ** (jnp.arange(half, dtype=F32) * (2.0 / QK_ROPE)))
    ang = pos.astype(F32)[:, None] * freqs[None, :]
    cos, sin = jnp.cos(ang), jnp.sin(ang)
    cos = jnp.tile(jnp.concatenate([cos, cos], axis=-1), (rows // pos.shape[0], N_HEADS))
    sin = jnp.tile(jnp.concatenate([-sin, sin], axis=-1), (rows // pos.shape[0], N_HEADS))
    return cos, sin


def _mla_weights(w_in, g_q, g_kv, w_uq, w_uk, w_uv, w_o):
    half = QK_ROPE // 2
    swap = jnp.concatenate([jnp.arange(half, QK_ROPE), jnp.arange(half)])
    c2 = Q_LORA + KV_LORA
    w_in_ext = jnp.concatenate([w_in, w_in[:, c2:][:, swap]], axis=1)
    uq = w_uq.reshape(Q_LORA, N_HEADS, QK_NOPE + QK_ROPE)
    rope_cols = uq[:, :, QK_NOPE:]
    a = jnp.transpose(w_uk, (1, 2, 0)).reshape(N_HEADS // 4, 4, QK_NOPE, KV_LORA)
    w_uk4 = jnp.einsum("pjnc,jk->pjnkc", a, jnp.eye(4, dtype=a.dtype)).reshape(N_HEADS // 4, 4 * QK_NOPE, 4 * KV_LORA)
    side = jax.nn.one_hot(jnp.arange(N_HEADS) % 2, 2, dtype=w_uv.dtype)
    w_uv2 = jnp.einsum("chv,hk->hckv", w_uv, side).reshape(N_HEADS, KV_LORA, 2 * V_HEAD)
    return {
        "w_in": w_in_ext.astype(BF16), "g_q": g_q.reshape(1, -1), "g_kv": g_kv.reshape(1, -1),
        "w_qn": uq[:, :, :QK_NOPE].reshape(Q_LORA, -1).astype(BF16),
        "w_qa": rope_cols.reshape(Q_LORA, -1).astype(BF16),
        "w_qb": rope_cols[:, :, swap].reshape(Q_LORA, -1).astype(BF16),
        "w_uk4": w_uk4.astype(BF16), "w_uv2": w_uv2.astype(BF16), "w_o": w_o.astype(BF16),
    }


def _router_weights(w_router, b_router):
    d = w_router.shape[0]
    wr = jnp.zeros((d, LANES), F32).at[:, :N_EXPERTS].set(w_router)
    hi = wr.astype(BF16)
    return {
        "r_hi": hi, "r_lo": (wr - hi.astype(F32)).astype(BF16),
        "r_b": jnp.zeros((1, LANES), F32).at[0, :N_EXPERTS].set(b_router),
    }


def _tile(n, want):
    if n <= want:
        return n
    t = want - want % 16
    while n % t:
        t -= 16
    return t


def kernel(x_prompt, x_sample, cache_ckv, cache_kpe, state_pool, page_table, meta_tokens, g_mix, g_ffn, g_final, pool_w, pool_scale, mla_w_in, mla_g_q, mla_g_kv, mla_w_uq, mla_w_uk, mla_w_uv, mla_w_o, ffn_w_gate, ffn_w_up, ffn_w_down, moe_w_router, moe_b_router, moe_w_gate, moe_w_up, moe_w_down):
    b, seq, d = x_prompt.shape
    db, s_len, _ = x_sample.shape
    depth = g_mix.shape[0]
    assert depth % 2 == 0
    past_len = page_table.shape[1] * cache_ckv.shape[2]
    n_main, n_samp = b * seq, db * s_len

    xm = x_prompt
    xs = x_sample
    xe = meta_tokens[None].astype(F32)

    tm_main = _tile(n_main, 1024)
    tm_samp = _tile(n_samp, 512)
    cos_m, sin_m = _rope_tables(N_META + jnp.arange(seq), max(seq, tm_main))
    cos_s, sin_s = _rope_tables(past_len + jnp.arange(s_len), tm_samp)
    cos_e, sin_e = _rope_tables(jnp.arange(N_META), N_META)

    experts = (moe_w_gate.astype(BF16), moe_w_up.astype(BF16), moe_w_down.astype(BF16))
    outs = {k: [] for k in ("ckv_p", "kpe_p", "pool_p", "ckv_s", "kpe_s", "pool_s")}
    for i in range(depth):
        j = i // 2
        last = i == depth - 1
        if i % 2 == 0:
            w_bf = pool_w[j].astype(BF16)
            zeros_e = jnp.zeros((1, HIST, d), F32)
            xe, tail_e = _pool_layer(xe, zeros_e, g_mix[i], w_bf, pool_scale[j], has_hist=False, tb=1, tl=N_META)
            xm, tail_m = _pool_layer(xm, tail_e, g_mix[i], w_bf, pool_scale[j], has_hist=True, tb=1,
                                     tl=min(512, seq))
            prev_s = jnp.concatenate([jnp.zeros((db, 1, d), F32), state_pool[j]], axis=1)
            xs, tail_s = _pool_layer(xs, prev_s, g_mix[i], w_bf, pool_scale[j], has_hist=True,
                                     tb=_tile(db, 64), tl=s_len)
            outs["pool_p"].append(tail_m[:, 1:])
            outs["pool_s"].append(tail_s[:, 1:])

            wg, wu, wd = ffn_w_gate[j].astype(BF16), ffn_w_up[j].astype(BF16), ffn_w_down[j].astype(BF16)
            xm = _ffn_dense(xm.reshape(n_main, d), g_ffn[i], wg, wu, wd, tm=tm_main).reshape(b, seq, d)
            small = jnp.concatenate([xs.reshape(n_samp, d), xe[0]], axis=0)
            small = _ffn_dense(small, g_ffn[i], wg, wu, wd, tm=small.shape[0])
            xs, xe = small[:n_samp].reshape(db, s_len, d), small[n_samp:][None]
        else:
            w = _mla_weights(mla_w_in[j], mla_g_q[j], mla_g_kv[j], mla_w_uq[j], mla_w_uk[j], mla_w_uv[j], mla_w_o[j])
            xe2, xm2, xs2 = xe[0], xm.reshape(n_main, d), xs.reshape(n_samp, d)
            ql_e, qp_e, ckv_e, kpe_e, ckvb_e, kpeb_e = _mla_proj(xe2, cos_e, sin_e, g_mix[i], w, tm=N_META, q_dtype=BF16)
            ql_m, qp_m, ckv_m, kpe_m, ckvb_m, kpeb_m = _mla_proj(xm2, cos_m, sin_m, g_mix[i], w, tm=tm_main, q_dtype=BF16)
            ql_s, qp_s, ckv_s, kpe_s, _, _ = _mla_proj(xs2, cos_s, sin_s, g_mix[i], w, tm=tm_samp, q_dtype=F32)

            o_e = _attn_causal(ql_e, qp_e, ckvb_e, kpeb_e, None, batch=1, tqb=N_META, tqs=N_META, tk=LANES)
            o_m = _attn_causal(ql_m, qp_m, ckvb_m, kpeb_m, (ckvb_e, kpeb_e), batch=b, tqb=min(256, seq), tqs=64,
                               tk=512)
            o_s = _attn_decode(ql_s, qp_s, ckv_s, kpe_s, cache_ckv, cache_kpe, page_table, layer=j, s_len=s_len)

            xe2 = _attn_out(o_e, xe2, w["w_uv2"], w["w_o"], tm=N_META)
            xm2 = _attn_out(o_m, xm2, w["w_uv2"], w["w_o"], tm=tm_main)
            xs2 = _attn_out(o_s, xs2, w["w_uv2"], w["w_o"], tm=tm_samp)

            outs["ckv_p"].append((ckv_e, ckv_m))
            outs["kpe_p"].append((kpe_e, kpe_m))
            outs["ckv_s"].append(ckv_s.reshape(db, s_len, KV_LORA))
            outs["kpe_s"].append(kpe_s.reshape(db, s_len, QK_ROPE))

            mw = _router_weights(moe_w_router[j], moe_b_router[j])
            small = jnp.concatenate([xs2, xe2], axis=0)
            xm, small = _moe([xm2, small], [_tile(n_main, 512), small.shape[0]], g_ffn[i], mw, experts, g_final,
                             layer=j, tile=512, final_norm=last)
            xm = xm.reshape(b, seq, d)
            xs, xe = small[:n_samp].reshape(db, s_len, d), small[n_samp:][None]

    def with_meta(pairs):
        n_l, width = len(pairs), pairs[0][0].shape[-1]
        meta = jnp.broadcast_to(jnp.stack([e for e, _ in pairs])[:, None], (n_l, b, N_META, width))
        main = jnp.stack([m for _, m in pairs]).reshape(n_l, b, seq, width)
        return jnp.concatenate([meta, main], axis=2)

    return (xm, xs, with_meta(outs["ckv_p"]), with_meta(outs["kpe_p"]), jnp.stack(outs["pool_p"]),
            jnp.stack(outs["ckv_s"]), jnp.stack(outs["kpe_s"]), jnp.stack(outs["pool_s"]))
```
